```python
import math
import jax
import jax.numpy as jnp
from jax import lax
import numpy as np

D_MODEL = 1024
BATCH = 2
SEQ = 8192
DEPTH = 4
DEC_BATCH = 128
DEC_SEQ = 4
PAST_LEN = 2048
PAGE_SIZE = 128

N_MIXERS = 4
D_FF = 2816
NORM_EPS = 1e-6
POOL_WINDOWS = (2, 4, 8, 16)
POOL_GROUP = D_MODEL // len(POOL_WINDOWS)
POOL_BUF = max(POOL_WINDOWS) - 1
D_RNN = D_MODEL
RG_BLOCKS = 4
RG_BLOCK = D_RNN // RG_BLOCKS
RG_CONV = 4
RG_C = 8.0
CV_WIDTH = 31
HEAD_DIM = 64
ATT_HEADS = D_MODEL // (2 * HEAD_DIM)
ATT_QK = ATT_HEADS * 2 * HEAD_DIM
ATT_V = ATT_HEADS * 2 * HEAD_DIM
ROPE_THETA = 10000.0
Q_BLOCK = 128
ATT_LAYER = 3
LAMBDA_INIT = 0.8 - 0.6 * math.exp(-0.3 * ATT_LAYER)

kernel_name = "interleaved_pool_rglru_conformer_diffattn_decoder_step"

F32 = jnp.float32


def _rmsnorm(x, g):
    xf = x.astype(F32)
    y = xf * lax.rsqrt(jnp.mean(xf * xf, axis=-1, keepdims=True) + NORM_EPS)
    return (y * g.astype(F32)).astype(x.dtype)


def _layernorm(x, g, b):
    xf = x.astype(F32)
    xc = xf - jnp.mean(xf, axis=-1, keepdims=True)
    y = xc * lax.rsqrt(jnp.mean(xc * xc, axis=-1, keepdims=True) + NORM_EPS)
    return (y * g.astype(F32) + b.astype(F32)).astype(x.dtype)


def _swiglu(u, wg, wu, wd):
    return (jax.nn.silu(u @ wg) * (u @ wu)) @ wd


def _causal_dwconv(x, prev, w, b):
    ext = jnp.concatenate([prev.astype(x.dtype), x], axis=1)
    y = lax.conv_general_dilated(ext, w[:, None, :].astype(x.dtype), window_strides=(1,), padding='VALID',
                                 dimension_numbers=('NWC', 'WIO', 'NWC'), feature_group_count=x.shape[-1])
    return y + b, ext[:, ext.shape[1] - (w.shape[0] - 1):]


def _pool_mixer(u, pos, prev, w, scale):
    B, T, _ = u.shape
    ext = jnp.concatenate([prev.astype(u.dtype), u], axis=1)
    cs = jnp.pad(jnp.cumsum(ext.astype(F32), axis=1), ((0, 0), (1, 0), (0, 0)))
    end = cs[:, POOL_BUF + 1:]
    groups = []
    for g, win in enumerate(POOL_WINDOWS):
        sl = slice(g * POOL_GROUP, (g + 1) * POOL_GROUP)
        start = cs[:, POOL_BUF + 1 - win:POOL_BUF + 1 - win + T, sl]
        cnt = jnp.minimum(pos + 1, win).astype(F32)[None, :, None]
        groups.append((end[..., sl] - start) / cnt)
    pooled = jnp.stack(groups, axis=2)
    diff = (pooled - u.astype(F32).reshape(B, T, len(POOL_WINDOWS), POOL_GROUP)).astype(u.dtype)
    y = jnp.einsum('btgc,gcd->btgd', diff, w).reshape(B, T, D_MODEL)
    return y * scale, ext[:, -POOL_BUF:]


def _linear_scan(a, b, h0):
    def step(h, ab):
        h = ab[0] * h + ab[1]
        return h, h
    hT, hs = lax.scan(step, h0, (jnp.swapaxes(a, 0, 1), jnp.swapaxes(b, 0, 1)))
    return jnp.swapaxes(hs, 0, 1), hT


def _rglru_mixer(u, conv_prev, h_prev, w_gate, w_in, conv_w, conv_b, wa, ba, wx, bx, lam, w_out):
    B, T, _ = u.shape
    gate = jax.nn.gelu(u @ w_gate)
    z, conv_new = _causal_dwconv(u @ w_in, conv_prev, conv_w, conv_b)
    zb = z.reshape(B, T, RG_BLOCKS, RG_BLOCK)
    r = jax.nn.sigmoid((jnp.einsum('btnc,ncd->btnd', zb, wa).reshape(B, T, D_RNN) + ba).astype(F32))
    i = jax.nn.sigmoid((jnp.einsum('btnc,ncd->btnd', zb, wx).reshape(B, T, D_RNN) + bx).astype(F32))
    log_a = -RG_C * r * jax.nn.softplus(-lam.astype(F32))
    a = jnp.exp(log_a)
    b = jnp.sqrt(-jnp.expm1(2.0 * log_a)) * (i * z.astype(F32))
    hs, hT = _linear_scan(a, b, h_prev.astype(F32))
    y = (gate * hs.astype(u.dtype)) @ w_out
    return y, conv_new, hT.astype(h_prev.dtype)


def _conv_module(u, prev, w1, b1, dw_w, dw_b, ln_g, ln_b, w2, b2):
    hcat = u @ w1 + b1
    g = hcat[..., :D_MODEL] * jax.nn.sigmoid(hcat[..., D_MODEL:])
    c, new = _causal_dwconv(g, prev, dw_w, dw_b)
    c = jax.nn.silu(_layernorm(c, ln_g, ln_b))
    return c @ w2 + b2, new


def _rope(x, pos):
    half = HEAD_DIM // 2
    inv = ROPE_THETA ** (-jnp.arange(half, dtype=F32) / half)
    ang = pos.astype(F32)[:, None] * inv[None, :]
    cos = jnp.cos(ang)[:, None, None, :]
    sin = jnp.sin(ang)[:, None, None, :]
    xf = x.astype(F32)
    x1, x2 = xf[..., :half], xf[..., half:]
    return jnp.concatenate([x1 * cos - x2 * sin, x2 * cos + x1 * sin], axis=-1).astype(x.dtype)


def _diff_combine(q, k, v, q_pos, k_pos, lam):
    s = jnp.einsum('bqhcd,bkhcd->bhcqk', q, k).astype(F32) * (HEAD_DIM ** -0.5)
    mask = k_pos[None, :] <= q_pos[:, None]
    p = jax.nn.softmax(jnp.where(mask, s, -1e30), axis=-1)
    att = p[:, :, 0] - lam * p[:, :, 1]
    return jnp.einsum('bhqk,bkhe->bqhe', att.astype(v.dtype), v)


def _diff_attn_mixer(u, pos, k_past, v_past, w_qkv, q_g, k_g, lq1, lk1, lq2, lk2, subln, w_o):
    B, T, _ = u.shape
    qkv = u @ w_qkv
    q = qkv[..., :ATT_QK].reshape(B, T, ATT_HEADS, 2, HEAD_DIM)
    k = qkv[..., ATT_QK:2 * ATT_QK].reshape(B, T, ATT_HEADS, 2, HEAD_DIM)
    v = qkv[..., 2 * ATT_QK:].reshape(B, T, ATT_HEADS, 2 * HEAD_DIM)
    q = _rope(_rmsnorm(q, q_g), pos)
    k = _rope(_rmsnorm(k, k_g), pos)
    lam = (jnp.exp(jnp.sum(lq1.astype(F32) * lk1.astype(F32)))
           - jnp.exp(jnp.sum(lq2.astype(F32) * lk2.astype(F32))) + LAMBDA_INIT)
    if k_past is None:
        nb = T // Q_BLOCK
        qb = jnp.swapaxes(q.reshape(B, nb, Q_BLOCK, ATT_HEADS, 2, HEAD_DIM), 0, 1)
        pb = pos.reshape(nb, Q_BLOCK)
        o = lax.map(lambda a: _diff_combine(a[0], k, v, a[1], pos, lam), (qb, pb))
        o = jnp.swapaxes(o, 0, 1).reshape(B, T, ATT_HEADS, 2 * HEAD_DIM)
    else:
        P = k_past.shape[1]
        k_all = jnp.concatenate([k_past.reshape(B, P, ATT_HEADS, 2, HEAD_DIM).astype(k.dtype), k], axis=1)
        v_all = jnp.concatenate([v_past.astype(v.dtype), v], axis=1)
        k_pos = jnp.concatenate([jnp.arange(P, dtype=jnp.int32), pos])
        o = _diff_combine(q, k_all, v_all, pos, k_pos, lam)
    o = _rmsnorm(o, subln) * (1.0 - LAMBDA_INIT)
    y = o.reshape(B, T, ATT_V) @ w_o
    return y, k.reshape(B, T, ATT_HEADS, 2 * HEAD_DIM), v


def _trunk(x, pos, pool_prev, rg_conv_prev, rg_h_prev, cv_prev, k_past, v_past, p):
    outs = {}
    for i in range(DEPTH):
        x = x + 0.5 * _swiglu(_rmsnorm(x, p['ffn1_norm'][i]), p['ffn1_w_gate'][i], p['ffn1_w_up'][i], p['ffn1_w_down'][i])
        u = _rmsnorm(x, p['mix_norm'][i])
        kind = i % N_MIXERS
        if kind == 0:
            m, outs['pool'] = _pool_mixer(u, pos, pool_prev, p['pool_w'], p['pool_scale'])
        elif kind == 1:
            m, outs['rgc'], outs['rgh'] = _rglru_mixer(
                u, rg_conv_prev, rg_h_prev, p['rg_w_gate'], p['rg_w_in'], p['rg_conv_w'], p['rg_conv_b'],
                p['rg_wa'], p['rg_ba'], p['rg_wx'], p['rg_bx'], p['rg_lambda'], p['rg_w_out'])
        elif kind == 2:
            m, outs['cv'] = _conv_module(u, cv_prev, p['cv_w_pw1'], p['cv_b_pw1'], p['cv_dw_w'], p['cv_dw_b'],
                                         p['cv_ln_g'], p['cv_ln_b'], p['cv_w_pw2'], p['cv_b_pw2'])
        else:
            m, outs['k'], outs['v'] = _diff_attn_mixer(
                u, pos, k_past, v_past, p['at_w_qkv'], p['at_q_norm'], p['at_k_norm'], p['at_lam_q1'],
                p['at_lam_k1'], p['at_lam_q2'], p['at_lam_k2'], p['at_subln'], p['at_w_o'])
        x = x + m
        x = x + 0.5 * _swiglu(_rmsnorm(x, p['ffn2_norm'][i]), p['ffn2_w_gate'][i], p['ffn2_w_up'][i], p['ffn2_w_down'][i])
    return x, outs['pool'], outs['rgc'], outs['rgh'], outs['cv'], outs['k'], outs['v']


def setup_inputs(seed: int = 0) -> dict:
    key = jax.random.key(seed)
    ks = iter(jax.random.split(key, 64))

    def nrm(shape, scale):
        return jax.random.normal(next(ks), shape, F32) * scale

    def gain(shape):
        return 1.0 + nrm(shape, 0.05)

    n_pages = PAST_LEN // PAGE_SIZE
    n_used = DEC_BATCH * n_pages
    n_phys = (5 * n_used) // 4
    perm = jax.random.permutation(next(ks), n_phys)
    page_table = perm[:n_used].reshape(DEC_BATCH, n_pages).astype(jnp.int32)
    a0 = jax.random.uniform(next(ks), (D_RNN,), F32, 0.9, 0.999) ** (1.0 / RG_C)
    rg_lambda = jnp.log(a0) - jnp.log1p(-a0)
    return {
        "x_prompt": nrm((BATCH, SEQ, D_MODEL), 1.0),
        "x_sample": nrm((DEC_BATCH, DEC_SEQ, D_MODEL), 1.0),
        "state_pool": nrm((DEC_BATCH, POOL_BUF, D_MODEL), 1.0),
        "state_rglru_conv": nrm((DEC_BATCH, RG_CONV - 1, D_RNN), 1.0),
        "state_rglru_h": nrm((DEC_BATCH, D_RNN), 0.5),
        "state_conv": nrm((DEC_BATCH, CV_WIDTH - 1, D_MODEL), 0.5),
        "cache_k": nrm((n_phys, PAGE_SIZE, ATT_HEADS, 2 * HEAD_DIM), 1.0),
        "cache_v": nrm((n_phys, PAGE_SIZE, ATT_HEADS, 2 * HEAD_DIM), 1.0),
        "page_table": page_table,
        "ffn1_norm": gain((DEPTH, D_MODEL)),
        "ffn1_w_gate": nrm((DEPTH, D_MODEL, D_FF), D_MODEL ** -0.5),
        "ffn1_w_up": nrm((DEPTH, D_MODEL, D_FF), D_MODEL ** -0.5),
        "ffn1_w_down": nrm((DEPTH, D_FF, D_MODEL), D_FF ** -0.5),
        "mix_norm": gain((DEPTH, D_MODEL)),
        "ffn2_norm": gain((DEPTH, D_MODEL)),
        "ffn2_w_gate": nrm((DEPTH, D_MODEL, D_FF), D_MODEL ** -0.5),
        "ffn2_w_up": nrm((DEPTH, D_MODEL, D_FF), D_MODEL ** -0.5),
        "ffn2_w_down": nrm((DEPTH, D_FF, D_MODEL), D_FF ** -0.5),
        "pool_w": nrm((len(POOL_WINDOWS), POOL_GROUP, POOL_GROUP), POOL_GROUP ** -0.5),
        "pool_scale": 1.0 + nrm((D_MODEL,), 0.1),
        "rg_w_gate": nrm((D_MODEL, D_RNN), D_MODEL ** -0.5),
        "rg_w_in": nrm((D_MODEL, D_RNN), D_MODEL ** -0.5),
        "rg_conv_w": nrm((RG_CONV, D_RNN), RG_CONV ** -0.5),
        "rg_conv_b": nrm((D_RNN,), 0.01),
        "rg_wa": nrm((RG_BLOCKS, RG_BLOCK, RG_BLOCK), RG_BLOCK ** -0.5),
        "rg_ba": nrm((D_RNN,), 0.01),
        "rg_wx": nrm((RG_BLOCKS, RG_BLOCK, RG_BLOCK), RG_BLOCK ** -0.5),
        "rg_bx": nrm((D_RNN,), 0.01),
        "rg_lambda": rg_lambda,
        "rg_w_out": nrm((D_RNN, D_MODEL), D_RNN ** -0.5),
        "cv_w_pw1": nrm((D_MODEL, 2 * D_MODEL), D_MODEL ** -0.5),
        "cv_b_pw1": nrm((2 * D_MODEL,), 0.01),
        "cv_dw_w": nrm((CV_WIDTH, D_MODEL), CV_WIDTH ** -0.5),
        "cv_dw_b": nrm((D_MODEL,), 0.01),
        "cv_ln_g": gain((D_MODEL,)),
        "cv_ln_b": nrm((D_MODEL,), 0.01),
        "cv_w_pw2": nrm((D_MODEL, D_MODEL), D_MODEL ** -0.5),
        "cv_b_pw2": nrm((D_MODEL,), 0.01),
        "at_w_qkv": nrm((D_MODEL, 2 * ATT_QK + ATT_V), D_MODEL ** -0.5),
        "at_q_norm": gain((HEAD_DIM,)),
        "at_k_norm": gain((HEAD_DIM,)),
        "at_lam_q1": nrm((HEAD_DIM,), 0.1),
        "at_lam_k1": nrm((HEAD_DIM,), 0.1),
        "at_lam_q2": nrm((HEAD_DIM,), 0.1),
        "at_lam_k2": nrm((HEAD_DIM,), 0.1),
        "at_subln": gain((2 * HEAD_DIM,)),
        "at_w_o": nrm((ATT_V, D_MODEL), ATT_V ** -0.5),
    }


def reference(x_prompt, x_sample, state_pool, state_rglru_conv, state_rglru_h, state_conv, cache_k, cache_v,
              page_table, ffn1_norm, ffn1_w_gate, ffn1_w_up, ffn1_w_down, mix_norm, ffn2_norm, ffn2_w_gate,
              ffn2_w_up, ffn2_w_down, pool_w, pool_scale, rg_w_gate, rg_w_in, rg_conv_w, rg_conv_b, rg_wa, rg_ba,
              rg_wx, rg_bx, rg_lambda, rg_w_out, cv_w_pw1, cv_b_pw1, cv_dw_w, cv_dw_b, cv_ln_g, cv_ln_b, cv_w_pw2,
              cv_b_pw2, at_w_qkv, at_q_norm, at_k_norm, at_lam_q1, at_lam_k1, at_lam_q2, at_lam_k2, at_subln, at_w_o):
    p = dict(ffn1_norm=ffn1_norm, ffn1_w_gate=ffn1_w_gate, ffn1_w_up=ffn1_w_up, ffn1_w_down=ffn1_w_down,
             mix_norm=mix_norm, ffn2_norm=ffn2_norm, ffn2_w_gate=ffn2_w_gate, ffn2_w_up=ffn2_w_up,
             ffn2_w_down=ffn2_w_down, pool_w=pool_w, pool_scale=pool_scale, rg_w_gate=rg_w_gate, rg_w_in=rg_w_in,
             rg_conv_w=rg_conv_w, rg_conv_b=rg_conv_b, rg_wa=rg_wa, rg_ba=rg_ba, rg_wx=rg_wx, rg_bx=rg_bx,
             rg_lambda=rg_lambda, rg_w_out=rg_w_out, cv_w_pw1=cv_w_pw1, cv_b_pw1=cv_b_pw1, cv_dw_w=cv_dw_w,
             cv_dw_b=cv_dw_b, cv_ln_g=cv_ln_g, cv_ln_b=cv_ln_b, cv_w_pw2=cv_w_pw2, cv_b_pw2=cv_b_pw2,
             at_w_qkv=at_w_qkv, at_q_norm=at_q_norm, at_k_norm=at_k_norm, at_lam_q1=at_lam_q1,
             at_lam_k1=at_lam_k1, at_lam_q2=at_lam_q2, at_lam_k2=at_lam_k2, at_subln=at_subln, at_w_o=at_w_o)
    Bp, Tp, _ = x_prompt.shape
    dt = x_prompt.dtype
    pos_p = jnp.arange(Tp, dtype=jnp.int32)
    y_p, pool_p, rgc_p, rgh_p, cv_p, k_p, v_p = _trunk(
        x_prompt, pos_p, jnp.zeros((Bp, POOL_BUF, D_MODEL), dt), jnp.zeros((Bp, RG_CONV - 1, D_RNN), dt),
        jnp.zeros((Bp, D_RNN), state_rglru_h.dtype), jnp.zeros((Bp, CV_WIDTH - 1, D_MODEL), dt), None, None, p)
    Bs, Ts, _ = x_sample.shape
    n_pages = page_table.shape[1]
    past = n_pages * PAGE_SIZE
    k_past = cache_k[page_table].reshape(Bs, past, ATT_HEADS, 2 * HEAD_DIM)
    v_past = cache_v[page_table].reshape(Bs, past, ATT_HEADS, 2 * HEAD_DIM)
    pos_s = past + jnp.arange(Ts, dtype=jnp.int32)
    y_s, pool_s, rgc_s, rgh_s, cv_s, k_s, v_s = _trunk(
        x_sample, pos_s, state_pool, state_rglru_conv, state_rglru_h, state_conv, k_past, v_past, p)
    return (y_p, y_s, pool_p, pool_s, rgc_p, rgc_s, rgh_p, rgh_s, cv_p, cv_s, k_p, v_p, k_s, v_s)
```

```python
import functools
import math

import jax
import jax.numpy as jnp
from jax import lax
from jax.experimental import pallas as pl
from jax.experimental.pallas import tpu as pltpu

F32 = jnp.float32
BF16 = jnp.bfloat16

NORM_EPS = 1e-6
POOL_WINDOWS = (2, 4, 8, 16)
POOL_BUF = max(POOL_WINDOWS) - 1
RG_CONV = 4
RG_C = 8.0
CV_WIDTH = 31
HEAD_DIM = 64
HEAD_WIDTH = 2 * HEAD_DIM
ROPE_THETA = 10000.0
ATT_LAYER = 3
LAMBDA_INIT = 0.8 - 0.6 * math.exp(-0.3 * ATT_LAYER)
NEG_INF = -1e30

SUBLANES = 8
VMEM_LIMIT = 56 * 1024 * 1024


def _params(sem):
    return pltpu.CompilerParams(dimension_semantics=sem, vmem_limit_bytes=VMEM_LIMIT)


def _rms(x, g):
    ms = jnp.mean(x * x, axis=-1, keepdims=True)
    return x * lax.rsqrt(ms + NORM_EPS) * g


def _dot(a, b):
    return jnp.dot(a, b, preferred_element_type=F32)


def _dot_nt(a, b):
    return lax.dot_general(a, b, (((1,), (1,)), ((), ())), preferred_element_type=F32)


def _const_spec(shape):
    n = len(shape)
    return pl.BlockSpec(shape, lambda *_: (0,) * n)


def _ffn_kernel(*refs, fc, with_proj):
    if with_proj:
        x_ref, a_ref, wo_ref, g_ref, wg_ref, wu_ref, wd_ref, o_ref, h_ref = refs
        x = x_ref[...] + _dot(a_ref[...], wo_ref[...])
    else:
        x_ref, g_ref, wg_ref, wu_ref, wd_ref, o_ref, h_ref = refs
        x = x_ref[...]
    u = _rms(x, g_ref[...]).astype(BF16)
    d_ff = wg_ref.shape[1]
    for c in range(d_ff // fc):
        sl = slice(c * fc, (c + 1) * fc)
        gate = _dot(u, wg_ref[:, sl])
        up = _dot(u, wu_ref[:, sl])
        h_ref[:, sl] = (gate * jax.nn.sigmoid(gate) * up).astype(BF16)
    o_ref[...] = x + 0.5 * _dot(h_ref[...], wd_ref[...])


def _ffn(x, g, wg, wu, wd, proj=None):
    n, d = x.shape
    d_ff = wg.shape[1]
    tm = min(512, n)
    row = pl.BlockSpec((tm, d), lambda i: (i, 0))
    single = pl.Buffered(1)
    args, specs = [x], [row]
    if proj is not None:
        a, wo = proj
        args += [a, wo]
        specs += [pl.BlockSpec((tm, a.shape[1]), lambda i: (i, 0)),
                  pl.BlockSpec(wo.shape, lambda i: (0, 0), pipeline_mode=single)]
    args += [g, wg, wu, wd]
    specs += [pl.BlockSpec((1, d), lambda i: (0, 0)),
              pl.BlockSpec((d, d_ff), lambda i: (0, 0), pipeline_mode=single),
              pl.BlockSpec((d, d_ff), lambda i: (0, 0), pipeline_mode=single),
              pl.BlockSpec((d_ff, d), lambda i: (0, 0), pipeline_mode=single)]
    return pl.pallas_call(
        functools.partial(_ffn_kernel, fc=256, with_proj=proj is not None),
        grid=(n // tm,),
        in_specs=specs,
        out_specs=row,
        out_shape=jax.ShapeDtypeStruct((n, d), F32),
        scratch_shapes=[pltpu.VMEM((tm, d_ff), BF16)],
        compiler_params=_params(("arbitrary",)),
        name="ffn_proj" if proj is not None else "ffn",
    )(*args)


def _load_history(ext_ref, prev_ref, halo_rows):
    pb = prev_ref.shape[0]
    if halo_rows > pb:
        ext_ref[0:halo_rows - pb, :] = jnp.zeros((halo_rows - pb, ext_ref.shape[1]), F32)
    ext_ref[halo_rows - pb:halo_rows, :] = prev_ref[...]


def _pool_kernel(x_ref, prev_ref, g_ref, w_ref, sc_ref, o_ref, st_ref, ext_ref, *, stride, tt, halo, pos0, nsteps):
    t = pl.program_id(1)
    rows, hr = tt * stride, halo * stride
    pb = prev_ref.shape[0]

    @pl.when(t == 0)
    def _():
        _load_history(ext_ref, prev_ref, hr)

    x = x_ref[...]
    u = _rms(x, g_ref[...])
    ext_ref[hr:hr + rows, :] = u
    step = lax.broadcasted_iota(jnp.int32, (rows, 1), 0) >> int(math.log2(stride))
    pos = pos0 + t * tt + step
    gw = x.shape[1] // len(POOL_WINDOWS)
    for gi, win in enumerate(POOL_WINDOWS):
        cs = slice(gi * gw, (gi + 1) * gw)
        s = u[:, cs]
        for k in range(1, win):
            s = s + ext_ref[hr - k * stride:hr - k * stride + rows, cs]
        cnt = jnp.minimum(pos + 1, win).astype(F32)
        diff = (s / cnt - u[:, cs]).astype(BF16)
        o_ref[:, cs] = x[:, cs] + _dot(diff, w_ref[gi]) * sc_ref[:, cs]

    @pl.when(t == nsteps - 1)
    def _():
        st_ref[...] = ext_ref[hr + rows - pb:hr + rows, :]

    if nsteps > 1:
        ext_ref[0:hr, :] = ext_ref[rows:rows + hr, :]


def _seq_specs(rows, d):
    return pl.BlockSpec((None, rows, d), lambda g, t: (g, t, 0))


def _state_spec(shape):
    return pl.BlockSpec((None,) + tuple(shape[1:]), lambda g, t: (g,) + (0,) * (len(shape) - 1))


def _tiling(x, stride):
    groups, r, d = x.shape
    tt = min(512, r // stride) if stride == 1 else r // stride
    rows = tt * stride
    return groups, r, d, tt, rows, r // rows


def _pool_mixer(x, prev, g, w, scale, stride, pos0):
    groups, r, d, tt, rows, nsteps = _tiling(x, stride)
    halo = 16 if stride == 1 else POOL_BUF
    return pl.pallas_call(
        functools.partial(_pool_kernel, stride=stride, tt=tt, halo=halo, pos0=pos0, nsteps=nsteps),
        grid=(groups, nsteps),
        in_specs=[_seq_specs(rows, d), _state_spec(prev.shape), _const_spec(g.shape), _const_spec(w.shape),
                  _const_spec(scale.shape)],
        out_specs=[_seq_specs(rows, d), _state_spec(prev.shape)],
        out_shape=[jax.ShapeDtypeStruct(x.shape, F32), jax.ShapeDtypeStruct(prev.shape, F32)],
        scratch_shapes=[pltpu.VMEM((halo * stride + rows, d), F32)],
        compiler_params=_params(("arbitrary", "arbitrary")),
        name="pool_mixer",
    )(x, prev, g, w, scale)


def _rglru_kernel(x_ref, cprev_ref, hprev_ref, g_ref, wgate_ref, win_ref, cw_ref, cb_ref, wa_ref, ba_ref, wx_ref,
                  bx_ref, lam_ref, wout_ref, o_ref, cst_ref, hst_ref, ext_ref, h_ref, a_ref, b_ref,
                  *, stride, tt, halo, nsteps):
    t = pl.program_id(1)
    rows, hr = tt * stride, halo * stride
    pb = cprev_ref.shape[0]

    @pl.when(t == 0)
    def _():
        _load_history(ext_ref, cprev_ref, hr)
        h_ref[...] = hprev_ref[...]

    x = x_ref[...]
    u = _rms(x, g_ref[...]).astype(BF16)
    gate = jax.nn.gelu(_dot(u, wgate_ref[...]))
    zin = _dot(u, win_ref[...])
    ext_ref[hr:hr + rows, :] = zin
    z = cb_ref[...] + cw_ref[RG_CONV - 1:RG_CONV, :] * zin
    for k in range(RG_CONV - 1):
        back = (RG_CONV - 1 - k) * stride
        z = z + cw_ref[k:k + 1, :] * ext_ref[hr - back:hr - back + rows, :]
    zb = z.astype(BF16)
    lam = lam_ref[...]
    softplus_neg_lam = jnp.maximum(-lam, 0.0) + jnp.log1p(jnp.exp(-jnp.abs(lam)))
    nblk = wa_ref.shape[0]
    bw = x.shape[1] // nblk
    for n in range(nblk):
        cs = slice(n * bw, (n + 1) * bw)
        r = jax.nn.sigmoid(_dot(zb[:, cs], wa_ref[n]) + ba_ref[:, cs])
        i = jax.nn.sigmoid(_dot(zb[:, cs], wx_ref[n]) + bx_ref[:, cs])
        log_a = -RG_C * r * softplus_neg_lam[:, cs]
        a = jnp.exp(log_a)
        a_ref[:, cs] = a
        b_ref[:, cs] = jnp.sqrt(1.0 - a * a) * (i * z[:, cs])

    if stride == 1:
        def chunk(c, h):
            r0 = pl.multiple_of(c * SUBLANES, SUBLANES)
            a = a_ref[pl.ds(r0, SUBLANES), :]
            b = b_ref[pl.ds(r0, SUBLANES), :]
            row = lax.broadcasted_iota(jnp.int32, a.shape, 0)
            for s in (1, 2, 4):
                keep = row >= s
                b = jnp.where(keep, a * pltpu.roll(b, s, axis=0) + b, b)
                a = jnp.where(keep, a * pltpu.roll(a, s, axis=0), a)
            hs = a * h + b
            b_ref[pl.ds(r0, SUBLANES), :] = hs
            return hs[SUBLANES - 1:SUBLANES, :]

        h_ref[...] = lax.fori_loop(0, rows // SUBLANES, chunk, h_ref[...], unroll=4)
    else:
        h = h_ref[...]
        for s in range(tt):
            sl = slice(s * stride, (s + 1) * stride)
            h = a_ref[sl, :] * h + b_ref[sl, :]
            b_ref[sl, :] = h
        h_ref[...] = h

    y = _dot((gate * b_ref[...]).astype(BF16), wout_ref[...])
    o_ref[...] = x + y

    @pl.when(t == nsteps - 1)
    def _():
        cst_ref[...] = ext_ref[hr + rows - pb:hr + rows, :]
        hst_ref[...] = h_ref[...]

    if nsteps > 1:
        ext_ref[0:hr, :] = ext_ref[rows:rows + hr, :]


def _rglru_mixer(x, cprev, hprev, g, wgate, win, cw, cb, wa, ba, wx, bx, lam, wout, stride):
    groups, r, d, tt, rows, nsteps = _tiling(x, stride)
    halo = SUBLANES if stride == 1 else RG_CONV - 1
    consts = [g, wgate, win, cw, cb, wa, ba, wx, bx, lam, wout]
    return pl.pallas_call(
        functools.partial(_rglru_kernel, stride=stride, tt=tt, halo=halo, nsteps=nsteps),
        grid=(groups, nsteps),
        in_specs=[_seq_specs(rows, d), _state_spec(cprev.shape), _state_spec(hprev.shape)]
                 + [_const_spec(c.shape) for c in consts],
        out_specs=[_seq_specs(rows, d), _state_spec(cprev.shape), _state_spec(hprev.shape)],
        out_shape=[jax.ShapeDtypeStruct(x.shape, F32), jax.ShapeDtypeStruct(cprev.shape, F32),
                   jax.ShapeDtypeStruct(hprev.shape, F32)],
        scratch_shapes=[pltpu.VMEM((halo * stride + rows, d), F32), pltpu.VMEM((stride, d), F32),
                        pltpu.VMEM((rows, d), F32), pltpu.VMEM((rows, d), F32)],
        compiler_params=_params(("arbitrary", "arbitrary")),
        name="rglru_mixer",
    )(x, cprev, hprev, *consts)


def _convmod_kernel(x_ref, prev_ref, g_ref, w1_ref, b1_ref, dw_ref, dwb_ref, lng_ref, lnb_ref, w2_ref, b2_ref,
                    o_ref, st_ref, ext_ref, *, stride, tt, halo, nsteps):
    t = pl.program_id(1)
    rows, hr = tt * stride, halo * stride
    pb = prev_ref.shape[0]
    d = x_ref.shape[1]

    @pl.when(t == 0)
    def _():
        _load_history(ext_ref, prev_ref, hr)

    x = x_ref[...]
    u = _rms(x, g_ref[...]).astype(BF16)
    hc = _dot(u, w1_ref[...]) + b1_ref[...]
    glu = hc[:, :d] * jax.nn.sigmoid(hc[:, d:])
    ext_ref[hr:hr + rows, :] = glu
    c = dwb_ref[...] + dw_ref[CV_WIDTH - 1:CV_WIDTH, :] * glu
    for k in range(CV_WIDTH - 1):
        back = (CV_WIDTH - 1 - k) * stride
        c = c + dw_ref[k:k + 1, :] * ext_ref[hr - back:hr - back + rows, :]
    cc = c - jnp.mean(c, axis=-1, keepdims=True)
    y = cc * lax.rsqrt(jnp.mean(cc * cc, axis=-1, keepdims=True) + NORM_EPS)
    y = y * lng_ref[...] + lnb_ref[...]
    y = (y * jax.nn.sigmoid(y)).astype(BF16)
    o_ref[...] = x + _dot(y, w2_ref[...]) + b2_ref[...]

    @pl.when(t == nsteps - 1)
    def _():
        st_ref[...] = ext_ref[hr + rows - pb:hr + rows, :]

    if nsteps > 1:
        ext_ref[0:hr, :] = ext_ref[rows:rows + hr, :]


def _conv_mixer(x, prev, g, w1, b1, dw, dwb, lng, lnb, w2, b2, stride):
    groups, r, d, tt, rows, nsteps = _tiling(x, stride)
    halo = 32 if stride == 1 else CV_WIDTH - 1
    consts = [g, w1, b1, dw, dwb, lng, lnb, w2, b2]
    return pl.pallas_call(
        functools.partial(_convmod_kernel, stride=stride, tt=tt, halo=halo, nsteps=nsteps),
        grid=(groups, nsteps),
        in_specs=[_seq_specs(rows, d), _state_spec(prev.shape)] + [_const_spec(c.shape) for c in consts],
        out_specs=[_seq_specs(rows, d), _state_spec(prev.shape)],
        out_shape=[jax.ShapeDtypeStruct(x.shape, F32), jax.ShapeDtypeStruct(prev.shape, F32)],
        scratch_shapes=[pltpu.VMEM((halo * stride + rows, d), F32)],
        compiler_params=_params(("arbitrary", "arbitrary")),
        name="conv_mixer",
    )(x, prev, g, w1, b1, dw, dwb, lng, lnb, w2, b2)


def _qkv_kernel(x_ref, g_ref, w_ref, qg_ref, kg_ref, cos_ref, sin_ref, seg_ref, q_ref, k_ref, v_ref, kb_ref, vb_ref):
    d = x_ref.shape[1]
    u = _rms(x_ref[...], g_ref[...]).astype(BF16)
    qkv = _dot(u, w_ref[...])
    cos, sin = cos_ref[...], sin_ref[...]
    seg = seg_ref[...]
    lane = lax.broadcasted_iota(jnp.int32, (1, HEAD_WIDTH), 1)
    first_half = (lane & (HEAD_DIM - 1)) < HEAD_DIM // 2

    def norm_rope(blk, gain):
        sq = blk * blk
        hi = sq.astype(BF16)
        lo = (sq - hi.astype(F32)).astype(BF16)
        ms = _dot(hi, seg) + _dot(lo, seg)
        y = blk * lax.rsqrt(ms + NORM_EPS) * gain
        partner = jnp.where(first_half, pltpu.roll(y, HEAD_WIDTH - HEAD_DIM // 2, axis=1),
                            pltpu.roll(y, HEAD_DIM // 2, axis=1))
        return y * cos + partner * sin

    for h in range(d // HEAD_WIDTH):
        cs = slice(h * HEAD_WIDTH, (h + 1) * HEAD_WIDTH)
        q = norm_rope(qkv[:, cs], qg_ref[...])
        q_ref[:, cs] = (q * (HEAD_DIM ** -0.5)).astype(BF16)
        k = norm_rope(qkv[:, d + h * HEAD_WIDTH:d + (h + 1) * HEAD_WIDTH], kg_ref[...])
        k_ref[:, cs] = k
        kb_ref[:, cs] = k.astype(BF16)
    v = qkv[:, 2 * d:]
    v_ref[...] = v
    vb_ref[...] = v.astype(BF16)


def _qkv(x, g, w, qg, kg, cos, sin, seg):
    n, d = x.shape
    tm = min(512, n)
    tab_blocks = cos.shape[0] // tm
    row = pl.BlockSpec((tm, d), lambda i: (i, 0))
    tab = pl.BlockSpec((tm, HEAD_WIDTH), lambda i: (i % tab_blocks, 0))
    return pl.pallas_call(
        _qkv_kernel,
        grid=(n // tm,),
        in_specs=[row, _const_spec(g.shape), _const_spec(w.shape), _const_spec(qg.shape), _const_spec(kg.shape),
                  tab, tab, _const_spec(seg.shape)],
        out_specs=[row] * 5,
        out_shape=[jax.ShapeDtypeStruct((n, d), dt) for dt in (BF16, F32, F32, BF16, BF16)],
        compiler_params=_params(("arbitrary",)),
        name="qkv_rope",
    )(x, g, w, qg, kg, cos, sin, seg)


def _finish_heads(o, subln):
    ms = jnp.mean(o * o, axis=-1, keepdims=True)
    return o * lax.rsqrt(ms + NORM_EPS) * subln * (1.0 - LAMBDA_INIT)


def _flash_kernel(lam_ref, q_ref, k_ref, v_ref, sub_ref, o_ref, m_ref, l_ref, acc_ref, *, tq, tk):
    i = pl.program_id(2)
    q = q_ref[...]
    lane = lax.broadcasted_iota(jnp.int32, q.shape, 1)
    qh = (jnp.where(lane < HEAD_DIM, q, jnp.zeros_like(q)), jnp.where(lane >= HEAD_DIM, q, jnp.zeros_like(q)))
    m_ref[...] = jnp.full(m_ref.shape, NEG_INF, F32)
    l_ref[...] = jnp.zeros(l_ref.shape, F32)
    acc_ref[...] = jnp.zeros(acc_ref.shape, F32)

    def block(j, masked):
        k0 = pl.multiple_of(j * tk, tk)
        k = k_ref[pl.ds(k0, tk), :]
        v = v_ref[pl.ds(k0, tk), :]
        if masked:
            row = lax.broadcasted_iota(jnp.int32, (tq, tk), 0) + i * tq
            col = lax.broadcasted_iota(jnp.int32, (tq, tk), 1) + j * tk
            visible = col <= row
        for c in range(2):
            s = _dot_nt(qh[c], k)
            if masked:
                s = jnp.where(visible, s, NEG_INF)
            m_old = m_ref[c]
            m_new = jnp.maximum(m_old, jnp.max(s, axis=1, keepdims=True))
            alpha = jnp.exp(m_old - m_new)
            p = jnp.exp(s - m_new)
            l_ref[c] = alpha * l_ref[c] + jnp.sum(p, axis=1, keepdims=True)
            acc_ref[c] = alpha * acc_ref[c] + _dot(p.astype(BF16), v)
            m_ref[c] = m_new

    def body(j, carry):
        block(j, False)
        return carry

    nfull = (i * tq) // tk
    lax.fori_loop(0, nfull, body, 0)
    for jj in range(tq // tk):
        block(nfull + jj, True)
    o = acc_ref[0] / l_ref[0] - lam_ref[...] * (acc_ref[1] / l_ref[1])
    o_ref[...] = _finish_heads(o, sub_ref[...]).astype(BF16)


def _flash_attention(lam, q, k, v, subln):
    b, t, d = q.shape
    tq = min(512, t)
    tk = tq
    nh = d // HEAD_WIDTH
    return pl.pallas_call(
        functools.partial(_flash_kernel, tq=tq, tk=tk),
        grid=(b, nh, t // tq),
        in_specs=[pl.BlockSpec((1, 1), lambda bi, h, i: (0, 0)),
                  pl.BlockSpec((None, tq, HEAD_WIDTH), lambda bi, h, i: (bi, i, h)),
                  pl.BlockSpec((None, t, HEAD_WIDTH), lambda bi, h, i: (bi, 0, h)),
                  pl.BlockSpec((None, t, HEAD_WIDTH), lambda bi, h, i: (bi, 0, h)),
                  pl.BlockSpec((1, HEAD_WIDTH), lambda bi, h, i: (0, 0))],
        out_specs=pl.BlockSpec((None, tq, HEAD_WIDTH), lambda bi, h, i: (bi, i, h)),
        out_shape=jax.ShapeDtypeStruct((b, t, d), BF16),
        scratch_shapes=[pltpu.VMEM((2, tq, 1), F32), pltpu.VMEM((2, tq, 1), F32),
                        pltpu.VMEM((2, tq, HEAD_WIDTH), F32)],
        compiler_params=_params(("arbitrary", "arbitrary", "arbitrary")),
        name="flash_diff_attention",
    )(lam, q, k, v, subln)


def _paged_kernel(pt_ref, lam_ref, qm_ref, knew_ref, vnew_ref, sub_ref, *refs, pages_per_step, nsteps, nq, nh):
    kp_refs = refs[:pages_per_step]
    vp_refs = refs[pages_per_step:2 * pages_per_step]
    o_ref, m_ref, l_ref, acc_ref = refs[2 * pages_per_step:]
    j = pl.program_id(1)
    nrow = qm_ref.shape[0]

    @pl.when(j == 0)
    def _():
        m_ref[...] = jnp.full(m_ref.shape, NEG_INF, F32)
        l_ref[...] = jnp.zeros(l_ref.shape, F32)
        acc_ref[...] = jnp.zeros(acc_ref.shape, F32)

    qm = qm_ref[...]

    def update(kmat, vmat, extra_mask):
        ncol = kmat.shape[0]
        s = _dot_nt(qm, kmat)
        row = lax.broadcasted_iota(jnp.int32, (nrow, ncol), 0)
        col = lax.broadcasted_iota(jnp.int32, (nrow, ncol), 1)
        valid = (col & (nh - 1)) == (row >> int(math.log2(2 * nq)))
        if extra_mask:
            valid = valid & ((col >> int(math.log2(nh))) <= (row & (nq - 1)))
        s = jnp.where(valid, s, NEG_INF)
        m_old = m_ref[...]
        m_new = jnp.maximum(m_old, jnp.max(s, axis=1, keepdims=True))
        alpha = jnp.exp(m_old - m_new)
        p = jnp.exp(s - m_new)
        l_ref[...] = alpha * l_ref[...] + jnp.sum(p, axis=1, keepdims=True)
        acc_ref[...] = alpha * acc_ref[...] + _dot(p.astype(BF16), vmat)
        m_ref[...] = m_new

    for p in range(pages_per_step):
        kp = kp_refs[p][...]
        vp = vp_refs[p][...]
        rows = kp.shape[0] * kp.shape[1]
        update(kp.reshape(rows, HEAD_WIDTH).astype(BF16), vp.reshape(rows, HEAD_WIDTH).astype(BF16), False)

    @pl.when(j == nsteps - 1)
    def _():
        kn = knew_ref[...]
        vn = vnew_ref[...]
        rows = kn.shape[0] * kn.shape[1]
        update(kn.reshape(rows, HEAD_WIDTH).astype(BF16), vn.reshape(rows, HEAD_WIDTH).astype(BF16), True)
        n = (acc_ref[...] / l_ref[...]).reshape(nh, 2 * nq, HEAD_WIDTH)
        o = n[:, 0:nq, :] - lam_ref[...] * n[:, nq:2 * nq, :]
        o_ref[...] = _finish_heads(o, sub_ref[...]).astype(BF16)


def _paged_attention(page_table, lam, qm, k_new, v_new, subln, cache_k, cache_v):
    bs, npages = page_table.shape
    _, nq, nh, _ = k_new.shape
    pages_per_step = 4 if npages % 4 == 0 else 1
    nsteps = npages // pages_per_step
    page_block = (None,) + cache_k.shape[1:]

    def page_spec(p):
        return pl.BlockSpec(page_block, lambda b, j, pt: (pt[b, j * pages_per_step + p], 0, 0, 0))

    seq4 = lambda shape: pl.BlockSpec((None,) + shape[1:], lambda b, j, pt: (b, 0, 0, 0))
    grid_spec = pltpu.PrefetchScalarGridSpec(
        num_scalar_prefetch=1,
        grid=(bs, nsteps),
        in_specs=[pl.BlockSpec((1, 1), lambda b, j, pt: (0, 0)),
                  pl.BlockSpec((None,) + qm.shape[1:], lambda b, j, pt: (b, 0, 0)),
                  seq4(k_new.shape), seq4(v_new.shape),
                  pl.BlockSpec((1, HEAD_WIDTH), lambda b, j, pt: (0, 0))]
                 + [page_spec(p) for p in range(pages_per_step)] * 2,
        out_specs=pl.BlockSpec((None, nh, nq, HEAD_WIDTH), lambda b, j, pt: (b, 0, 0, 0)),
        scratch_shapes=[pltpu.VMEM((qm.shape[1], 1), F32), pltpu.VMEM((qm.shape[1], 1), F32),
                        pltpu.VMEM((qm.shape[1], HEAD_WIDTH), F32)],
    )
    return pl.pallas_call(
        functools.partial(_paged_kernel, pages_per_step=pages_per_step, nsteps=nsteps, nq=nq, nh=nh),
        grid_spec=grid_spec,
        out_shape=jax.ShapeDtypeStruct((bs, nh, nq, HEAD_WIDTH), BF16),
        compiler_params=_params(("arbitrary", "arbitrary")),
        name="paged_diff_attention",
    )(page_table, lam, qm, k_new, v_new, subln, *([cache_k] * pages_per_step), *([cache_v] * pages_per_step))


def _lam_kernel(q1_ref, k1_ref, q2_ref, k2_ref, o_ref):
    dot1 = jnp.sum(q1_ref[...] * k1_ref[...], axis=-1, keepdims=True)
    dot2 = jnp.sum(q2_ref[...] * k2_ref[...], axis=-1, keepdims=True)
    o_ref[...] = jnp.exp(dot1) - jnp.exp(dot2) + LAMBDA_INIT


def _rope_tables(pos):
    half = HEAD_DIM // 2
    inv = ROPE_THETA ** (-jnp.arange(half, dtype=F32) / half)
    ang = pos.astype(F32)[:, None] * inv[None, :]
    cos, sin = jnp.cos(ang), jnp.sin(ang)
    return jnp.tile(cos, (1, 4)), jnp.tile(jnp.concatenate([-sin, sin], axis=1), (1, 2))


def _trunk(x, stride, pos0, pool_prev, rgc_prev, rgh_prev, cv_prev, attend, p):
    groups, r, d = x.shape
    outs = {}
    flat = lambda a: a.reshape(groups * r, d)
    pending = None
    for i in range(4):
        x = _ffn(flat(x), p['ffn1_norm'][i], p['ffn1_w_gate'][i], p['ffn1_w_up'][i], p['ffn1_w_down'][i],
                 proj=pending).reshape(groups, r, d)
        pending = None
        g = p['mix_norm'][i]
        if i == 0:
            x, outs['pool'] = _pool_mixer(x, pool_prev, g, p['pool_w'], p['pool_scale'], stride, pos0)
        elif i == 1:
            x, outs['rgc'], outs['rgh'] = _rglru_mixer(
                x, rgc_prev, rgh_prev, g, p['rg_w_gate'], p['rg_w_in'], p['rg_conv_w'], p['rg_conv_b'], p['rg_wa'],
                p['rg_ba'], p['rg_wx'], p['rg_bx'], p['rg_lambda'], p['rg_w_out'], stride)
        elif i == 2:
            x, outs['cv'] = _conv_mixer(x, cv_prev, g, p['cv_w_pw1'], p['cv_b_pw1'], p['cv_dw_w'], p['cv_dw_b'],
                                        p['cv_ln_g'], p['cv_ln_b'], p['cv_w_pw2'], p['cv_b_pw2'], stride)
        else:
            pos = pos0 + jnp.arange(r, dtype=jnp.int32) // stride
            cos, sin = _rope_tables(pos)
            q, k, v, kb, vb = _qkv(flat(x), g, p['at_w_qkv'], p['at_q_norm'], p['at_k_norm'], cos, sin, p['seg'])
            outs['k'], outs['v'] = k, v
            pending = (attend(q, k, v, kb, vb), p['at_w_o'])
        x = _ffn(flat(x), p['ffn2_norm'][i], p['ffn2_w_gate'][i], p['ffn2_w_up'][i], p['ffn2_w_down'][i],
                 proj=pending).reshape(groups, r, d)
        pending = None
    return x, outs


def kernel(x_prompt, x_sample, state_pool, state_rglru_conv, state_rglru_h, state_conv, cache_k, cache_v, page_table, ffn1_norm, ffn1_w_gate, ffn1_w_up, ffn1_w_down, mix_norm, ffn2_norm, ffn2_w_gate, ffn2_w_up, ffn2_w_down, pool_w, pool_scale, rg_w_gate, rg_w_in, rg_conv_w, rg_conv_b, rg_wa, rg_ba, rg_wx, rg_bx, rg_lambda, rg_w_out, cv_w_pw1, cv_b_pw1, cv_dw_w, cv_dw_b, cv_ln_g, cv_ln_b, cv_w_pw2, cv_b_pw2, at_w_qkv, at_q_norm, at_k_norm, at_lam_q1, at_lam_k1, at_lam_q2, at_lam_k2, at_subln, at_w_o):
    bp, tp, d = x_prompt.shape
    bs, ts, _ = x_sample.shape
    nh = d // HEAD_WIDTH
    past = page_table.shape[1] * cache_k.shape[1]
    bf = lambda w: w.astype(BF16)
    row = lambda v: v.reshape(1, -1).astype(F32)
    rows = lambda v: v.reshape(v.shape[0], 1, -1).astype(F32)

    seg_id = jnp.arange(HEAD_WIDTH) // HEAD_DIM
    p = dict(
        ffn1_norm=rows(ffn1_norm), ffn1_w_gate=bf(ffn1_w_gate), ffn1_w_up=bf(ffn1_w_up), ffn1_w_down=bf(ffn1_w_down),
        mix_norm=rows(mix_norm),
        ffn2_norm=rows(ffn2_norm), ffn2_w_gate=bf(ffn2_w_gate), ffn2_w_up=bf(ffn2_w_up), ffn2_w_down=bf(ffn2_w_down),
        pool_w=bf(pool_w), pool_scale=row(pool_scale),
        rg_w_gate=bf(rg_w_gate), rg_w_in=bf(rg_w_in), rg_conv_w=rg_conv_w, rg_conv_b=row(rg_conv_b),
        rg_wa=bf(rg_wa), rg_ba=row(rg_ba), rg_wx=bf(rg_wx), rg_bx=row(rg_bx), rg_lambda=row(rg_lambda),
        rg_w_out=bf(rg_w_out),
        cv_w_pw1=bf(cv_w_pw1), cv_b_pw1=row(cv_b_pw1), cv_dw_w=cv_dw_w, cv_dw_b=row(cv_dw_b), cv_ln_g=row(cv_ln_g),
        cv_ln_b=row(cv_ln_b), cv_w_pw2=bf(cv_w_pw2), cv_b_pw2=row(cv_b_pw2),
        at_w_qkv=bf(at_w_qkv), at_q_norm=row(jnp.tile(at_q_norm, 2)), at_k_norm=row(jnp.tile(at_k_norm, 2)),
        at_w_o=bf(at_w_o),
        seg=((seg_id[:, None] == seg_id[None, :]).astype(F32) / HEAD_DIM).astype(BF16),
    )
    subln = row(at_subln)
    lam = pl.pallas_call(
        _lam_kernel,
        in_specs=[_const_spec((1, HEAD_DIM))] * 4,
        out_specs=_const_spec((1, 1)),
        out_shape=jax.ShapeDtypeStruct((1, 1), F32),
        name="diff_lambda",
    )(row(at_lam_q1), row(at_lam_k1), row(at_lam_q2), row(at_lam_k2))

    def attend_prompt(q, k, v, kb, vb):
        shape = (bp, tp, d)
        return _flash_attention(lam, q.reshape(shape), kb.reshape(shape), vb.reshape(shape), subln).reshape(-1, d)

    zeros = lambda n: jnp.zeros((bp, n, d), F32)
    y_p, o_p = _trunk(x_prompt, 1, 0, zeros(POOL_BUF), zeros(RG_CONV - 1), zeros(1), zeros(CV_WIDTH - 1),
                      attend_prompt, p)

    def to_tm(a):
        return jnp.swapaxes(a, 0, 1).reshape(1, a.shape[1] * bs, d)

    def from_tm(a):
        return jnp.swapaxes(a.reshape(-1, bs, d), 0, 1)

    def attend_sample(q, k, v, kb, vb):
        del kb, vb
        q5 = from_tm(q).reshape(bs, ts, nh, 2, HEAD_DIM)
        eye = jnp.eye(2, dtype=q.dtype)
        qm = jnp.einsum('bqhcd,ce->bhcqed', q5, eye).reshape(bs, nh * 2 * ts, HEAD_WIDTH)
        k_new = from_tm(k).reshape(bs, ts, nh, HEAD_WIDTH)
        v_new = from_tm(v).reshape(bs, ts, nh, HEAD_WIDTH)
        o = _paged_attention(page_table, lam, qm, k_new, v_new, subln, cache_k, cache_v)
        return jnp.transpose(o, (2, 0, 1, 3)).reshape(ts * bs, d)

    y_s, o_s = _trunk(to_tm(x_sample), bs, past, to_tm(state_pool), to_tm(state_rglru_conv),
                      state_rglru_h.reshape(1, bs, d), to_tm(state_conv), attend_sample, p)

    kv4 = lambda a, b, t: a.reshape(b, t, nh, HEAD_WIDTH)
    return (y_p, from_tm(y_s),
            o_p['pool'], from_tm(o_s['pool']),
            o_p['rgc'], from_tm(o_s['rgc']),
            o_p['rgh'].reshape(bp, d), o_s['rgh'].reshape(bs, d),
            o_p['cv'], from_tm(o_s['cv']),
            kv4(o_p['k'], bp, tp), kv4(o_p['v'], bp, tp),
            kv4(from_tm(o_s['k']), bs, ts), kv4(from_tm(o_s['v']), bs, ts))
```

```python
import functools
import math

import jax
import jax.numpy as jnp
from jax import lax
from jax.experimental import pallas as pl
from jax.experimental.pallas import tpu as pltpu

F32 = jnp.float32
BF16 = jnp.bfloat16

NORM_EPS = 1e-6
POOL_WINDOWS = (2, 4, 8, 16)
POOL_BUF = max(POOL_WINDOWS) - 1
RG_CONV = 4
RG_C = 8.0
CV_WIDTH = 31
HEAD_DIM = 64
HEAD_WIDTH = 2 * HEAD_DIM
ROPE_THETA = 10000.0
ATT_LAYER = 3
LAMBDA_INIT = 0.8 - 0.6 * math.exp(-0.3 * ATT_LAYER)
NEG_INF = -1e30

Q_SCALE = HEAD_DIM ** -0.5 * math.log2(math.e)

SUBLANES = 8
VMEM_LIMIT = 56 * 1024 * 1024
PAGED_VMEM_BUDGET = 36 * 1024 * 1024
SEQ_TILE = 512
FFN_CHUNK = 256
VT_ROWS = HEAD_WIDTH + 16


def _params(sem):
    return pltpu.CompilerParams(dimension_semantics=sem, vmem_limit_bytes=VMEM_LIMIT)


def _rms(x, g):
    ms = jnp.mean(x * x, axis=-1, keepdims=True)
    return x * lax.rsqrt(ms + NORM_EPS) * g


def _dot(a, b):
    return jnp.dot(a, b, preferred_element_type=F32)


def _dot_nt(a, b):
    return lax.dot_general(a, b, (((1,), (1,)), ((), ())), preferred_element_type=F32)


def _const_spec(shape):
    n = len(shape)
    return pl.BlockSpec(shape, lambda *_: (0,) * n)


def _ffn_kernel(*refs, nstreams, tiles, with_proj):
    per = 2 if with_proj else 1
    ins = refs[:nstreams * per]
    rest = refs[nstreams * per:]
    if with_proj:
        wo_ref, rest = rest[0], rest[1:]
    g_ref, wg_ref, wu_ref, wd_ref = rest[:4]
    outs = rest[4:4 + nstreams]
    h_ref = rest[4 + nstreams]
    i = pl.program_id(0)
    d_ff = wg_ref.shape[1]
    start = 0
    for s in range(nstreams):
        @pl.when((i >= start) & (i < start + tiles[s]))
        def _(s=s):
            x = ins[s * per][...]
            if with_proj:
                x = x + _dot(ins[s * per + 1][...], wo_ref[...])
            u = _rms(x, g_ref[...]).astype(BF16)
            rows = x.shape[0]
            for c in range(d_ff // FFN_CHUNK):
                sl = slice(c * FFN_CHUNK, (c + 1) * FFN_CHUNK)
                gate = _dot(u, wg_ref[:, sl])
                up = _dot(u, wu_ref[:, sl])
                h_ref[0:rows, sl] = (gate * jax.nn.sigmoid(gate) * up).astype(BF16)
            outs[s][...] = x + 0.5 * _dot(h_ref[0:rows, :], wd_ref[...])
        start += tiles[s]


def _ffn(xs, g, wg, wu, wd, proj=None):
    d = xs[0].shape[1]
    d_ff = wg.shape[1]
    tms = [min(SEQ_TILE, x.shape[0]) for x in xs]
    tiles = [x.shape[0] // tm for x, tm in zip(xs, tms)]
    offsets = [sum(tiles[:s]) for s in range(len(xs))]
    single = pl.Buffered(1)

    def stream_spec(s, width):
        return pl.BlockSpec((tms[s], width),
                            lambda i, s=s: (jnp.clip(i - offsets[s], 0, tiles[s] - 1), 0))

    args, specs = [], []
    for s, x in enumerate(xs):
        args.append(x)
        specs.append(stream_spec(s, d))
        if proj is not None:
            args.append(proj[0][s])
            specs.append(stream_spec(s, proj[0][s].shape[1]))
    if proj is not None:
        args.append(proj[1])
        specs.append(pl.BlockSpec(proj[1].shape, lambda i: (0, 0), pipeline_mode=single))
    args += [g, wg, wu, wd]
    specs += [pl.BlockSpec((1, d), lambda i: (0, 0)),
              pl.BlockSpec((d, d_ff), lambda i: (0, 0), pipeline_mode=single),
              pl.BlockSpec((d, d_ff), lambda i: (0, 0), pipeline_mode=single),
              pl.BlockSpec((d_ff, d), lambda i: (0, 0), pipeline_mode=single)]
    return pl.pallas_call(
        functools.partial(_ffn_kernel, nstreams=len(xs), tiles=tuple(tiles), with_proj=proj is not None),
        grid=(sum(tiles),),
        in_specs=specs,
        out_specs=[stream_spec(s, d) for s in range(len(xs))],
        out_shape=[jax.ShapeDtypeStruct(x.shape, F32) for x in xs],
        scratch_shapes=[pltpu.VMEM((max(tms), d_ff), BF16)],
        compiler_params=_params(("arbitrary",)),
        name="ffn_proj" if proj is not None else "ffn",
    )(*args)


def _load_history(ext_ref, prev_ref, halo_rows):
    pb = prev_ref.shape[0]
    if halo_rows > pb:
        ext_ref[0:halo_rows - pb, :] = jnp.zeros((halo_rows - pb, ext_ref.shape[1]), F32)
    ext_ref[halo_rows - pb:halo_rows, :] = prev_ref[...]


def _pool_kernel(x_ref, prev_ref, g_ref, w_ref, sc_ref, o_ref, st_ref, ext_ref, *, stride, tt, halo, pos0, nsteps):
    t = pl.program_id(1)
    rows, hr = tt * stride, halo * stride
    pb = prev_ref.shape[0]

    @pl.when(t == 0)
    def _():
        _load_history(ext_ref, prev_ref, hr)

    x = x_ref[...]
    u = _rms(x, g_ref[...])
    ext_ref[hr:hr + rows, :] = u
    step = lax.broadcasted_iota(jnp.int32, (rows, 1), 0) >> int(math.log2(stride))
    pos = pos0 + t * tt + step
    gw = x.shape[1] // len(POOL_WINDOWS)
    for gi, win in enumerate(POOL_WINDOWS):
        cs = slice(gi * gw, (gi + 1) * gw)
        s = u[:, cs]
        for k in range(1, win):
            s = s + ext_ref[hr - k * stride:hr - k * stride + rows, cs]
        cnt = jnp.minimum(pos + 1, win).astype(F32)
        diff = (s / cnt - u[:, cs]).astype(BF16)
        o_ref[:, cs] = x[:, cs] + _dot(diff, w_ref[gi]) * sc_ref[:, cs]

    @pl.when(t == nsteps - 1)
    def _():
        st_ref[...] = ext_ref[hr + rows - pb:hr + rows, :]

    if nsteps > 1:
        ext_ref[0:hr, :] = ext_ref[rows:rows + hr, :]


def _seq_specs(rows, d):
    return pl.BlockSpec((None, rows, d), lambda g, t: (g, t, 0))


def _state_spec(shape):
    return pl.BlockSpec((None,) + tuple(shape[1:]), lambda g, t: (g,) + (0,) * (len(shape) - 1))


def _tiling(x, stride):
    groups, r, d = x.shape
    tt = min(SEQ_TILE, r) if stride == 1 else r // stride
    rows = tt * stride
    return groups, r, d, tt, rows, r // rows


def _pool_mixer(x, prev, g, w, scale, stride, pos0):
    groups, r, d, tt, rows, nsteps = _tiling(x, stride)
    halo = 16 if stride == 1 else POOL_BUF
    return pl.pallas_call(
        functools.partial(_pool_kernel, stride=stride, tt=tt, halo=halo, pos0=pos0, nsteps=nsteps),
        grid=(groups, nsteps),
        in_specs=[_seq_specs(rows, d), _state_spec(prev.shape), _const_spec(g.shape), _const_spec(w.shape),
                  _const_spec(scale.shape)],
        out_specs=[_seq_specs(rows, d), _state_spec(prev.shape)],
        out_shape=[jax.ShapeDtypeStruct(x.shape, F32), jax.ShapeDtypeStruct(prev.shape, F32)],
        scratch_shapes=[pltpu.VMEM((halo * stride + rows, d), F32)],
        compiler_params=_params(("arbitrary", "arbitrary")),
        name="pool_mixer",
    )(x, prev, g, w, scale)


def _rglru_kernel(x_ref, cprev_ref, hprev_ref, g_ref, wgate_ref, win_ref, cw_ref, cb_ref, wa_ref, ba_ref, wx_ref,
                  bx_ref, lam_ref, wout_ref, o_ref, cst_ref, hst_ref, ext_ref, h_ref, a_ref, b_ref,
                  *, stride, tt, halo, nsteps):
    t = pl.program_id(1)
    rows, hr = tt * stride, halo * stride
    pb = cprev_ref.shape[0]

    @pl.when(t == 0)
    def _():
        _load_history(ext_ref, cprev_ref, hr)
        h_ref[...] = hprev_ref[...]

    x = x_ref[...]
    u = _rms(x, g_ref[...]).astype(BF16)
    gate = jax.nn.gelu(_dot(u, wgate_ref[...]))
    zin = _dot(u, win_ref[...])
    ext_ref[hr:hr + rows, :] = zin
    z = cb_ref[...] + cw_ref[RG_CONV - 1:RG_CONV, :] * zin
    for k in range(RG_CONV - 1):
        back = (RG_CONV - 1 - k) * stride
        z = z + cw_ref[k:k + 1, :] * ext_ref[hr - back:hr - back + rows, :]
    zb = z.astype(BF16)
    lam = lam_ref[...]
    softplus_neg_lam = jnp.maximum(-lam, 0.0) + jnp.log1p(jnp.exp(-jnp.abs(lam)))
    nblk = wa_ref.shape[0]
    bw = x.shape[1] // nblk
    for n in range(nblk):
        cs = slice(n * bw, (n + 1) * bw)
        r = jax.nn.sigmoid(_dot(zb[:, cs], wa_ref[n]) + ba_ref[:, cs])
        i = jax.nn.sigmoid(_dot(zb[:, cs], wx_ref[n]) + bx_ref[:, cs])
        log_a = -RG_C * r * softplus_neg_lam[:, cs]
        a = jnp.exp(log_a)
        a_ref[:, cs] = a
        b_ref[:, cs] = jnp.sqrt(1.0 - a * a) * (i * z[:, cs])

    if stride == 1:
        def chunk(c, h):
            r0 = pl.multiple_of(c * SUBLANES, SUBLANES)
            a = a_ref[pl.ds(r0, SUBLANES), :]
            b = b_ref[pl.ds(r0, SUBLANES), :]
            row = lax.broadcasted_iota(jnp.int32, a.shape, 0)
            for s in (1, 2, 4):
                keep = row >= s
                b = jnp.where(keep, a * pltpu.roll(b, s, axis=0) + b, b)
                a = jnp.where(keep, a * pltpu.roll(a, s, axis=0), a)
            hs = a * h + b
            b_ref[pl.ds(r0, SUBLANES), :] = hs
            return hs[SUBLANES - 1:SUBLANES, :]

        h_ref[...] = lax.fori_loop(0, rows // SUBLANES, chunk, h_ref[...], unroll=4)
    else:
        h = h_ref[...]
        for s in range(tt):
            sl = slice(s * stride, (s + 1) * stride)
            h = a_ref[sl, :] * h + b_ref[sl, :]
            b_ref[sl, :] = h
        h_ref[...] = h

    y = _dot((gate * b_ref[...]).astype(BF16), wout_ref[...])
    o_ref[...] = x + y

    @pl.when(t == nsteps - 1)
    def _():
        cst_ref[...] = ext_ref[hr + rows - pb:hr + rows, :]
        hst_ref[...] = h_ref[...]

    if nsteps > 1:
        ext_ref[0:hr, :] = ext_ref[rows:rows + hr, :]


def _rglru_mixer(x, cprev, hprev, g, wgate, win, cw, cb, wa, ba, wx, bx, lam, wout, stride):
    groups, r, d, tt, rows, nsteps = _tiling(x, stride)
    halo = SUBLANES if stride == 1 else RG_CONV - 1
    consts = [g, wgate, win, cw, cb, wa, ba, wx, bx, lam, wout]
    return pl.pallas_call(
        functools.partial(_rglru_kernel, stride=stride, tt=tt, halo=halo, nsteps=nsteps),
        grid=(groups, nsteps),
        in_specs=[_seq_specs(rows, d), _state_spec(cprev.shape), _state_spec(hprev.shape)]
                 + [_const_spec(c.shape) for c in consts],
        out_specs=[_seq_specs(rows, d), _state_spec(cprev.shape), _state_spec(hprev.shape)],
        out_shape=[jax.ShapeDtypeStruct(x.shape, F32), jax.ShapeDtypeStruct(cprev.shape, F32),
                   jax.ShapeDtypeStruct(hprev.shape, F32)],
        scratch_shapes=[pltpu.VMEM((halo * stride + rows, d), F32), pltpu.VMEM((stride, d), F32),
                        pltpu.VMEM((rows, d), F32), pltpu.VMEM((rows, d), F32)],
        compiler_params=_params(("arbitrary", "arbitrary")),
        name="rglru_mixer",
    )(x, cprev, hprev, *consts)


def _convmod_kernel(x_ref, prev_ref, g_ref, w1_ref, b1_ref, dw_ref, dwb_ref, lng_ref, lnb_ref, w2_ref, b2_ref,
                    o_ref, st_ref, ext_ref, *, stride, tt, halo, nsteps):
    t = pl.program_id(1)
    rows, hr = tt * stride, halo * stride
    pb = prev_ref.shape[0]
    d = x_ref.shape[1]

    @pl.when(t == 0)
    def _():
        _load_history(ext_ref, prev_ref, hr)

    x = x_ref[...]
    u = _rms(x, g_ref[...]).astype(BF16)
    hc = _dot(u, w1_ref[...]) + b1_ref[...]
    glu = hc[:, :d] * jax.nn.sigmoid(hc[:, d:])
    ext_ref[hr:hr + rows, :] = glu
    c = dwb_ref[...]
    if stride == 1:
        for b in range(SUBLANES):
            phase = None
            for a in range((CV_WIDTH - 1 - b) // SUBLANES + 1):
                k = CV_WIDTH - 1 - (SUBLANES * a + b)
                start = hr - SUBLANES * (a + 1)
                term = dw_ref[k:k + 1, :] * ext_ref[start:start + rows + SUBLANES, :]
                phase = term if phase is None else phase + term
            c = c + phase[SUBLANES - b:SUBLANES - b + rows, :]
    else:
        for k in range(CV_WIDTH):
            back = (CV_WIDTH - 1 - k) * stride
            c = c + dw_ref[k:k + 1, :] * ext_ref[hr - back:hr - back + rows, :]
    cc = c - jnp.mean(c, axis=-1, keepdims=True)
    y = cc * lax.rsqrt(jnp.mean(cc * cc, axis=-1, keepdims=True) + NORM_EPS)
    y = y * lng_ref[...] + lnb_ref[...]
    y = (y * jax.nn.sigmoid(y)).astype(BF16)
    o_ref[...] = x + _dot(y, w2_ref[...]) + b2_ref[...]

    @pl.when(t == nsteps - 1)
    def _():
        st_ref[...] = ext_ref[hr + rows - pb:hr + rows, :]

    if nsteps > 1:
        ext_ref[0:hr, :] = ext_ref[rows:rows + hr, :]


def _conv_mixer(x, prev, g, w1, b1, dw, dwb, lng, lnb, w2, b2, stride):
    groups, r, d, tt, rows, nsteps = _tiling(x, stride)
    halo = 32 if stride == 1 else CV_WIDTH - 1
    consts = [g, w1, b1, dw, dwb, lng, lnb, w2, b2]
    return pl.pallas_call(
        functools.partial(_convmod_kernel, stride=stride, tt=tt, halo=halo, nsteps=nsteps),
        grid=(groups, nsteps),
        in_specs=[_seq_specs(rows, d), _state_spec(prev.shape)] + [_const_spec(c.shape) for c in consts],
        out_specs=[_seq_specs(rows, d), _state_spec(prev.shape)],
        out_shape=[jax.ShapeDtypeStruct(x.shape, F32), jax.ShapeDtypeStruct(prev.shape, F32)],
        scratch_shapes=[pltpu.VMEM((halo * stride + rows, d), F32)],
        compiler_params=_params(("arbitrary", "arbitrary")),
        name="conv_mixer",
    )(x, prev, g, w1, b1, dw, dwb, lng, lnb, w2, b2)


def _qkv_kernel(x_ref, g_ref, w_ref, qg_ref, kg_ref, cos_ref, sin_ref, seg_ref, q_ref, k_ref, v_ref, *flash_refs):
    d = x_ref.shape[1]
    u = _rms(x_ref[...], g_ref[...]).astype(BF16)
    qkv = _dot(u, w_ref[...])
    cos, sin = cos_ref[...], sin_ref[...]
    seg = seg_ref[...]
    lane = lax.broadcasted_iota(jnp.int32, (1, HEAD_WIDTH), 1)
    first_half = (lane & (HEAD_DIM - 1)) < HEAD_DIM // 2

    def norm_rope(blk, gain):
        sq = blk * blk
        hi = sq.astype(BF16)
        lo = (sq - hi.astype(F32)).astype(BF16)
        ms = _dot(hi, seg) + _dot(lo, seg)
        y = blk * lax.rsqrt(ms + NORM_EPS) * gain
        partner = jnp.where(first_half, pltpu.roll(y, HEAD_WIDTH - HEAD_DIM // 2, axis=1),
                            pltpu.roll(y, HEAD_DIM // 2, axis=1))
        return y * cos + partner * sin

    for h in range(d // HEAD_WIDTH):
        cs = slice(h * HEAD_WIDTH, (h + 1) * HEAD_WIDTH)
        q = norm_rope(qkv[:, cs], qg_ref[...])
        q_ref[:, cs] = (q * Q_SCALE).astype(BF16)
        k = norm_rope(qkv[:, d + h * HEAD_WIDTH:d + (h + 1) * HEAD_WIDTH], kg_ref[...])
        k_ref[:, cs] = k
        v = qkv[:, 2 * d + h * HEAD_WIDTH:2 * d + (h + 1) * HEAD_WIDTH]
        v_ref[:, cs] = v
        if flash_refs:
            kb_ref, vt_ref = flash_refs
            kb_ref[:, cs] = k.astype(BF16)
            vt_ref[h, 0:HEAD_WIDTH, :] = v.T.astype(BF16)
            pad = lax.broadcasted_iota(jnp.int32, (VT_ROWS - HEAD_WIDTH, v.shape[0]), 0)
            vt_ref[h, HEAD_WIDTH:VT_ROWS, :] = (pad == 0).astype(BF16)


def _qkv(x, g, w, qg, kg, cos, sin, seg, for_flash):
    n, d = x.shape
    tm = min(SEQ_TILE, n)
    nblk = cos.shape[0] // tm
    nh = d // HEAD_WIDTH
    row = pl.BlockSpec((tm, d), lambda i: (i, 0))
    tab = pl.BlockSpec((tm, HEAD_WIDTH), lambda i: (i % nblk, 0))
    out_specs = [row] * 3
    out_shape = [jax.ShapeDtypeStruct((n, d), dt) for dt in (BF16, F32, F32)]
    if for_flash:
        out_specs += [row, pl.BlockSpec((None, nh, None, VT_ROWS, tm), lambda i: (i // nblk, 0, i % nblk, 0, 0))]
        out_shape += [jax.ShapeDtypeStruct((n, d), BF16),
                      jax.ShapeDtypeStruct((n // (nblk * tm), nh, nblk, VT_ROWS, tm), BF16)]
    return pl.pallas_call(
        _qkv_kernel,
        grid=(n // tm,),
        in_specs=[row, _const_spec(g.shape), _const_spec(w.shape), _const_spec(qg.shape), _const_spec(kg.shape),
                  tab, tab, _const_spec(seg.shape)],
        out_specs=out_specs,
        out_shape=out_shape,
        compiler_params=_params(("arbitrary",)),
        name="qkv_rope",
    )(x, g, w, qg, kg, cos, sin, seg)


def _finish_heads(o, subln):
    ms = jnp.mean(o * o, axis=-1, keepdims=True)
    return o * lax.rsqrt(ms + NORM_EPS) * subln * (1.0 - LAMBDA_INIT)


def _flash_kernel(lam_ref, q_ref, k_ref, vt_ref, sub_ref, o_ref, acc_ref, sa_ref, sb_ref, m_ref, *, tile):
    i = pl.program_id(2)
    q = q_ref[...]
    lane = lax.broadcasted_iota(jnp.int32, q.shape, 1)
    qh = (jnp.where(lane < HEAD_DIM, q, jnp.zeros_like(q)), jnp.where(lane >= HEAD_DIM, q, jnp.zeros_like(q)))
    acc_ref[...] = jnp.zeros(acc_ref.shape, F32)
    m_ref[...] = jnp.full(m_ref.shape, NEG_INF, F32)

    def produce(j, s_ref):
        k = k_ref[pl.ds(pl.multiple_of(j * tile, tile), tile), :]
        for c in range(2):
            s_ref[c] = _dot_nt(k, qh[c])

    def consume(j, s_ref, masked):
        vt = vt_ref[j]
        if masked:
            visible = (lax.broadcasted_iota(jnp.int32, (tile, tile), 0)
                       <= lax.broadcasted_iota(jnp.int32, (tile, tile), 1))
        for c in range(2):
            st = s_ref[c]
            if masked:
                st = jnp.where(visible, st, NEG_INF)
            m_old = m_ref[c]
            m_new = jnp.maximum(m_old, jnp.max(st, axis=0, keepdims=True))
            pt = jnp.exp2(st - m_new).astype(BF16)
            acc_ref[c] = jnp.exp2(m_old - m_new) * acc_ref[c] + _dot(vt, pt)
            m_ref[c] = m_new

    produce(0, sa_ref)

    def body(j, carry):
        @pl.when(j % 2 == 0)
        def _():
            produce(j + 1, sb_ref)
            consume(j, sa_ref, False)

        @pl.when(j % 2 == 1)
        def _():
            produce(j + 1, sa_ref)
            consume(j, sb_ref, False)
        return carry

    lax.fori_loop(0, i, body, 0)

    @pl.when(i % 2 == 0)
    def _():
        consume(i, sa_ref, True)

    @pl.when(i % 2 == 1)
    def _():
        consume(i, sb_ref, True)

    a0, a1 = acc_ref[0], acc_ref[1]
    ot = (a0[:HEAD_WIDTH] / a0[HEAD_WIDTH:HEAD_WIDTH + 1]
          - lam_ref[...] * (a1[:HEAD_WIDTH] / a1[HEAD_WIDTH:HEAD_WIDTH + 1]))
    ms = jnp.mean(ot * ot, axis=0, keepdims=True)
    ot = ot * lax.rsqrt(ms + NORM_EPS) * sub_ref[...] * (1.0 - LAMBDA_INIT)
    o_ref[...] = ot.T.astype(BF16)


def _flash_attention(lam, q, k, vt, subln_col):
    b, t, d = q.shape
    nh, nblk, vrows, tile = vt.shape[1:]
    return pl.pallas_call(
        functools.partial(_flash_kernel, tile=tile),
        grid=(b, nh, nblk),
        in_specs=[pl.BlockSpec((1, 1), lambda bi, h, i: (0, 0)),
                  pl.BlockSpec((None, tile, HEAD_WIDTH), lambda bi, h, i: (bi, i, h)),
                  pl.BlockSpec((None, t, HEAD_WIDTH), lambda bi, h, i: (bi, 0, h)),
                  pl.BlockSpec((None, None, nblk, vrows, tile), lambda bi, h, i: (bi, h, 0, 0, 0)),
                  pl.BlockSpec((HEAD_WIDTH, 1), lambda bi, h, i: (0, 0))],
        out_specs=pl.BlockSpec((None, tile, HEAD_WIDTH), lambda bi, h, i: (bi, i, h)),
        out_shape=jax.ShapeDtypeStruct((b, t, d), BF16),
        scratch_shapes=[pltpu.VMEM((2, vrows, tile), F32), pltpu.VMEM((2, tile, tile), F32),
                        pltpu.VMEM((2, tile, tile), F32), pltpu.VMEM((2, 1, tile), F32)],
        compiler_params=_params(("arbitrary", "arbitrary", "arbitrary")),
        name="flash_diff_attention",
    )(lam, q, k, vt, subln_col)


def _lane_groups(x):
    return [x[:, g * HEAD_WIDTH:(g + 1) * HEAD_WIDTH] for g in range(x.shape[1] // HEAD_WIDTH)]


def _paged_kernel(pt_ref, lam_ref, qm_ref, knew_ref, vnew_ref, sub_ref, bias_ref, nbias_ref, *refs, npages, nq, nh):
    kp_refs, vp_refs = refs[:npages], refs[npages:2 * npages]
    o_ref, s_ref = refs[2 * npages:]
    qm = qm_ref[...]
    flat = lambda ref: ref[...].reshape(-1, HEAD_WIDTH).astype(BF16)

    kn = flat(knew_ref)
    s_new = _dot_nt(qm, kn) + nbias_ref[...]
    top = None
    for p in range(npages):
        s = _dot_nt(qm, flat(kp_refs[p])) + bias_ref[...]
        s_ref[p] = s
        page_top = functools.reduce(jnp.maximum, _lane_groups(s))
        top = page_top if top is None else jnp.maximum(top, page_top)
    m = jnp.maximum(jnp.max(top, axis=1, keepdims=True), jnp.max(s_new, axis=1, keepdims=True))
    m_lanes = jnp.broadcast_to(m, (m.shape[0], HEAD_WIDTH))

    p_new = jnp.exp2(s_new - m)
    acc = _dot(p_new.astype(BF16), flat(vnew_ref))
    lsum = None
    for p in range(npages):
        groups = [jnp.exp2(g - m_lanes) for g in _lane_groups(s_ref[p])]
        part = functools.reduce(jnp.add, groups)
        lsum = part if lsum is None else lsum + part
        acc = acc + _dot(jnp.concatenate(groups, axis=1).astype(BF16), flat(vp_refs[p]))
    l = jnp.sum(lsum, axis=1, keepdims=True) + jnp.sum(p_new, axis=1, keepdims=True)
    n = (acc / l).reshape(nh, 2 * nq, HEAD_WIDTH)
    o = n[:, 0:nq, :] - lam_ref[...] * n[:, nq:2 * nq, :]
    o_ref[...] = _finish_heads(o, sub_ref[...]).astype(BF16)


def _paged_attention(page_table, lam, qm, k_new, v_new, subln, cache_k, cache_v):
    bs, npages = page_table.shape
    _, nq, nh, _ = k_new.shape
    page_keys = cache_k.shape[1]
    nrow = qm.shape[1]
    page_bytes = page_keys * nh * HEAD_WIDTH * 4
    assert 2 * 2 * npages * page_bytes <= PAGED_VMEM_BUDGET, "all pages of a sequence must fit one grid step"

    row_head = jnp.arange(nrow)[:, None] // (2 * nq)
    row_query = jnp.arange(nrow)[:, None] % nq
    col = jnp.arange(page_keys * nh)[None, :]
    bias = jnp.where(col % nh == row_head, 0.0, NEG_INF).astype(F32)
    ncol = jnp.arange(nq * nh)[None, :]
    nbias = jnp.where((ncol % nh == row_head) & (ncol // nh <= row_query), 0.0, NEG_INF).astype(F32)

    page_block = (None,) + cache_k.shape[1:]
    page_spec = lambda p: pl.BlockSpec(page_block, lambda b, pt: (pt[b, p], 0, 0, 0))
    seq4 = lambda shape: pl.BlockSpec((None,) + shape[1:], lambda b, pt: (b, 0, 0, 0))
    const2 = lambda shape: pl.BlockSpec(shape, lambda b, pt: (0, 0))
    grid_spec = pltpu.PrefetchScalarGridSpec(
        num_scalar_prefetch=1,
        grid=(bs,),
        in_specs=[const2((1, 1)),
                  pl.BlockSpec((None,) + qm.shape[1:], lambda b, pt: (b, 0, 0)),
                  seq4(k_new.shape), seq4(v_new.shape), const2((1, HEAD_WIDTH)), const2(bias.shape),
                  const2(nbias.shape)]
                 + [page_spec(p) for p in range(npages)] * 2,
        out_specs=pl.BlockSpec((None, nh, nq, HEAD_WIDTH), lambda b, pt: (b, 0, 0, 0)),
        scratch_shapes=[pltpu.VMEM((npages, nrow, page_keys * nh), F32)],
    )
    return pl.pallas_call(
        functools.partial(_paged_kernel, npages=npages, nq=nq, nh=nh),
        grid_spec=grid_spec,
        out_shape=jax.ShapeDtypeStruct((bs, nh, nq, HEAD_WIDTH), BF16),
        compiler_params=_params(("arbitrary",)),
        name="paged_diff_attention",
    )(page_table, lam, qm, k_new, v_new, subln, bias, nbias, *([cache_k] * npages), *([cache_v] * npages))


def _lam_kernel(q1_ref, k1_ref, q2_ref, k2_ref, o_ref):
    dot1 = jnp.sum(q1_ref[...] * k1_ref[...], axis=-1, keepdims=True)
    dot2 = jnp.sum(q2_ref[...] * k2_ref[...], axis=-1, keepdims=True)
    o_ref[...] = jnp.exp(dot1) - jnp.exp(dot2) + LAMBDA_INIT


def _rope_tables(pos):
    half = HEAD_DIM // 2
    inv = ROPE_THETA ** (-jnp.arange(half, dtype=F32) / half)
    ang = pos.astype(F32)[:, None] * inv[None, :]
    cos, sin = jnp.cos(ang), jnp.sin(ang)
    return jnp.tile(cos, (1, 4)), jnp.tile(jnp.concatenate([-sin, sin], axis=1), (1, 2))


def _mixer(i, x, stride, pos0, st, attend, p):
    groups, r, d = x.shape
    g = p['mix_norm'][i]
    if i == 0:
        x, pool = _pool_mixer(x, st['pool'], g, p['pool_w'], p['pool_scale'], stride, pos0)
        return x, dict(pool=pool), None
    if i == 1:
        x, rgc, rgh = _rglru_mixer(
            x, st['rgc'], st['rgh'], g, p['rg_w_gate'], p['rg_w_in'], p['rg_conv_w'], p['rg_conv_b'], p['rg_wa'],
            p['rg_ba'], p['rg_wx'], p['rg_bx'], p['rg_lambda'], p['rg_w_out'], stride)
        return x, dict(rgc=rgc, rgh=rgh), None
    if i == 2:
        x, cv = _conv_mixer(x, st['cv'], g, p['cv_w_pw1'], p['cv_b_pw1'], p['cv_dw_w'], p['cv_dw_b'],
                            p['cv_ln_g'], p['cv_ln_b'], p['cv_w_pw2'], p['cv_b_pw2'], stride)
        return x, dict(cv=cv), None
    pos = pos0 + jnp.arange(r, dtype=jnp.int32) // stride
    cos, sin = _rope_tables(pos)
    q, k, v, *flash_kv = _qkv(x.reshape(groups * r, d), g, p['at_w_qkv'], p['at_q_norm'], p['at_k_norm'], cos, sin,
                              p['seg'], for_flash=stride == 1)
    return x, dict(k=k, v=v), attend(q, k, v, *flash_kv)


def _trunk(groups, p):
    xs = [grp['x'] for grp in groups]
    outs = [dict() for _ in groups]
    flat = lambda a: a.reshape(-1, a.shape[-1])
    for i in range(4):
        ys = _ffn([flat(x) for x in xs], p['ffn1_norm'][i], p['ffn1_w_gate'][i], p['ffn1_w_up'][i],
                  p['ffn1_w_down'][i])
        xs = [y.reshape(x.shape) for x, y in zip(xs, ys)]
        heads = []
        for n, grp in enumerate(groups):
            xs[n], new_state, pending = _mixer(i, xs[n], grp['stride'], grp['pos0'], grp['state'], grp['attend'], p)
            outs[n].update(new_state)
            heads.append(pending)
        proj = (heads, p['at_w_o']) if heads[0] is not None else None
        ys = _ffn([flat(x) for x in xs], p['ffn2_norm'][i], p['ffn2_w_gate'][i], p['ffn2_w_up'][i],
                  p['ffn2_w_down'][i], proj=proj)
        xs = [y.reshape(x.shape) for x, y in zip(xs, ys)]
    return xs, outs


def kernel(x_prompt, x_sample, state_pool, state_rglru_conv, state_rglru_h, state_conv, cache_k, cache_v, page_table, ffn1_norm, ffn1_w_gate, ffn1_w_up, ffn1_w_down, mix_norm, ffn2_norm, ffn2_w_gate, ffn2_w_up, ffn2_w_down, pool_w, pool_scale, rg_w_gate, rg_w_in, rg_conv_w, rg_conv_b, rg_wa, rg_ba, rg_wx, rg_bx, rg_lambda, rg_w_out, cv_w_pw1, cv_b_pw1, cv_dw_w, cv_dw_b, cv_ln_g, cv_ln_b, cv_w_pw2, cv_b_pw2, at_w_qkv, at_q_norm, at_k_norm, at_lam_q1, at_lam_k1, at_lam_q2, at_lam_k2, at_subln, at_w_o):
    bp, tp, d = x_prompt.shape
    bs, ts, _ = x_sample.shape
    nh = d // HEAD_WIDTH
    past = page_table.shape[1] * cache_k.shape[1]
    bf = lambda w: w.astype(BF16)
    row = lambda v: v.reshape(1, -1).astype(F32)
    rows = lambda v: v.reshape(v.shape[0], 1, -1).astype(F32)

    seg_id = jnp.arange(HEAD_WIDTH) // HEAD_DIM
    p = dict(
        ffn1_norm=rows(ffn1_norm), ffn1_w_gate=bf(ffn1_w_gate), ffn1_w_up=bf(ffn1_w_up), ffn1_w_down=bf(ffn1_w_down),
        mix_norm=rows(mix_norm),
        ffn2_norm=rows(ffn2_norm), ffn2_w_gate=bf(ffn2_w_gate), ffn2_w_up=bf(ffn2_w_up), ffn2_w_down=bf(ffn2_w_down),
        pool_w=bf(pool_w), pool_scale=row(pool_scale),
        rg_w_gate=bf(rg_w_gate), rg_w_in=bf(rg_w_in), rg_conv_w=rg_conv_w, rg_conv_b=row(rg_conv_b),
        rg_wa=bf(rg_wa), rg_ba=row(rg_ba), rg_wx=bf(rg_wx), rg_bx=row(rg_bx), rg_lambda=row(rg_lambda),
        rg_w_out=bf(rg_w_out),
        cv_w_pw1=bf(cv_w_pw1), cv_b_pw1=row(cv_b_pw1), cv_dw_w=cv_dw_w, cv_dw_b=row(cv_dw_b), cv_ln_g=row(cv_ln_g),
        cv_ln_b=row(cv_ln_b), cv_w_pw2=bf(cv_w_pw2), cv_b_pw2=row(cv_b_pw2),
        at_w_qkv=bf(at_w_qkv), at_q_norm=row(jnp.tile(at_q_norm, 2)), at_k_norm=row(jnp.tile(at_k_norm, 2)),
        at_w_o=bf(at_w_o),
        seg=((seg_id[:, None] == seg_id[None, :]).astype(F32) / HEAD_DIM).astype(BF16),
    )
    subln = row(at_subln)
    lam = pl.pallas_call(
        _lam_kernel,
        in_specs=[_const_spec((1, HEAD_DIM))] * 4,
        out_specs=_const_spec((1, 1)),
        out_shape=jax.ShapeDtypeStruct((1, 1), F32),
        name="diff_lambda",
    )(row(at_lam_q1), row(at_lam_k1), row(at_lam_q2), row(at_lam_k2))

    def attend_prompt(q, k, v, kb, vt):
        shape = (bp, tp, d)
        return _flash_attention(lam, q.reshape(shape), kb.reshape(shape), vt, subln.reshape(-1, 1)).reshape(-1, d)

    zeros = lambda n: jnp.zeros((bp, n, d), F32)
    prompt = dict(x=x_prompt, stride=1, pos0=0, attend=attend_prompt,
                  state=dict(pool=zeros(POOL_BUF), rgc=zeros(RG_CONV - 1), rgh=zeros(1), cv=zeros(CV_WIDTH - 1)))

    def to_tm(a):
        return jnp.swapaxes(a, 0, 1).reshape(1, a.shape[1] * bs, d)

    def from_tm(a):
        return jnp.swapaxes(a.reshape(-1, bs, d), 0, 1)

    def attend_sample(q, k, v):
        q5 = from_tm(q).reshape(bs, ts, nh, 2, HEAD_DIM)
        eye = jnp.eye(2, dtype=q.dtype)
        qm = jnp.einsum('bqhcd,ce->bhcqed', q5, eye).reshape(bs, nh * 2 * ts, HEAD_WIDTH)
        k_new = from_tm(k).reshape(bs, ts, nh, HEAD_WIDTH)
        v_new = from_tm(v).reshape(bs, ts, nh, HEAD_WIDTH)
        o = _paged_attention(page_table, lam, qm, k_new, v_new, subln, cache_k, cache_v)
        return jnp.transpose(o, (2, 0, 1, 3)).reshape(ts * bs, d)

    sample = dict(x=to_tm(x_sample), stride=bs, pos0=past, attend=attend_sample,
                  state=dict(pool=to_tm(state_pool), rgc=to_tm(state_rglru_conv),
                             rgh=state_rglru_h.reshape(1, bs, d), cv=to_tm(state_conv)))
    (y_p, y_s), (o_p, o_s) = _trunk([prompt, sample], p)

    kv4 = lambda a, b, t: a.reshape(b, t, nh, HEAD_WIDTH)
    return (y_p, from_tm(y_s),
            o_p['pool'], from_tm(o_s['pool']),
            o_p['rgc'], from_tm(o_s['rgc']),
            o_p['rgh'].reshape(bp, d), o_s['rgh'].reshape(bs, d),
            o_p['cv'], from_tm(o_s['cv']),
            kv4(o_p['k'], bp, tp), kv4(o_p['v'], bp, tp),
            kv4(from_tm(o_s['k']), bs, ts), kv4(from_tm(o_s['v']), bs, ts))
```

```python
import functools
import math

import jax
import jax.numpy as jnp
from jax import lax
from jax.experimental import pallas as pl
from jax.experimental.pallas import tpu as pltpu

F32 = jnp.float32
BF16 = jnp.bfloat16

NORM_EPS = 1e-6
POOL_WINDOWS = (2, 4, 8, 16)
POOL_BUF = max(POOL_WINDOWS) - 1
RG_CONV = 4
RG_C = 8.0
CV_WIDTH = 31
HEAD_DIM = 64
HEAD_WIDTH = 2 * HEAD_DIM
ROPE_THETA = 10000.0
ATT_LAYER = 3
LAMBDA_INIT = 0.8 - 0.6 * math.exp(-0.3 * ATT_LAYER)
NEG_INF = -1e30

Q_SCALE = HEAD_DIM ** -0.5 * math.log2(math.e)

SUBLANES = 8
VMEM_LIMIT = 56 * 1024 * 1024
PAGED_VMEM_BUDGET = 36 * 1024 * 1024
SEQ_TILE = 512
FFN_CHUNK = 256
VT_ROWS = HEAD_WIDTH + 16


def _params(sem):
    return pltpu.CompilerParams(dimension_semantics=sem, vmem_limit_bytes=VMEM_LIMIT)


def _rms(x, g):
    ms = jnp.mean(x * x, axis=-1, keepdims=True)
    return x * lax.rsqrt(ms + NORM_EPS) * g


def _dot(a, b):
    return jnp.dot(a, b, preferred_element_type=F32)


def _dot_nt(a, b):
    return lax.dot_general(a, b, (((1,), (1,)), ((), ())), preferred_element_type=F32)


def _const_spec(shape):
    n = len(shape)
    return pl.BlockSpec(shape, lambda *_: (0,) * n)


def _ffn_kernel(*refs, nstreams, tiles, with_proj, nffn):
    per = 2 if with_proj else 1
    ins = refs[:nstreams * per]
    rest = refs[nstreams * per:]
    if with_proj:
        wo_ref, rest = rest[0], rest[1:]
    ffns = [rest[4 * n:4 * n + 4] for n in range(nffn)]
    outs = rest[4 * nffn:4 * nffn + nstreams]
    h_ref = rest[4 * nffn + nstreams]
    i = pl.program_id(0)
    start = 0
    for s in range(nstreams):
        @pl.when((i >= start) & (i < start + tiles[s]))
        def _(s=s):
            x = ins[s * per][...]
            if with_proj:
                x = x + _dot(ins[s * per + 1][...], wo_ref[...])
            rows = x.shape[0]
            for g_ref, wg_ref, wu_ref, wd_ref in ffns:
                u = _rms(x, g_ref[...]).astype(BF16)
                for c in range(wg_ref.shape[1] // FFN_CHUNK):
                    sl = slice(c * FFN_CHUNK, (c + 1) * FFN_CHUNK)
                    gate = _dot(u, wg_ref[:, sl])
                    up = _dot(u, wu_ref[:, sl])
                    h_ref[0:rows, sl] = (gate * jax.nn.sigmoid(gate) * up).astype(BF16)
                x = x + 0.5 * _dot(h_ref[0:rows, :], wd_ref[...])
            outs[s][...] = x
        start += tiles[s]


def _ffn(xs, ffns, proj=None):
    d = xs[0].shape[1]
    d_ff = ffns[0][1].shape[1]
    tms = [min(SEQ_TILE, x.shape[0]) for x in xs]
    tiles = [x.shape[0] // tm for x, tm in zip(xs, tms)]
    offsets = [sum(tiles[:s]) for s in range(len(xs))]
    single = pl.Buffered(1)

    def stream_spec(s, width):
        return pl.BlockSpec((tms[s], width),
                            lambda i, s=s: (jnp.clip(i - offsets[s], 0, tiles[s] - 1), 0))

    args, specs = [], []
    for s, x in enumerate(xs):
        args.append(x)
        specs.append(stream_spec(s, d))
        if proj is not None:
            args.append(proj[0][s])
            specs.append(stream_spec(s, proj[0][s].shape[1]))
    if proj is not None:
        args.append(proj[1])
        specs.append(pl.BlockSpec(proj[1].shape, lambda i: (0, 0), pipeline_mode=single))
    for g, wg, wu, wd in ffns:
        args += [g, wg, wu, wd]
        specs += [pl.BlockSpec((1, d), lambda i: (0, 0)),
                  pl.BlockSpec((d, d_ff), lambda i: (0, 0), pipeline_mode=single),
                  pl.BlockSpec((d, d_ff), lambda i: (0, 0), pipeline_mode=single),
                  pl.BlockSpec((d_ff, d), lambda i: (0, 0), pipeline_mode=single)]
    return pl.pallas_call(
        functools.partial(_ffn_kernel, nstreams=len(xs), tiles=tuple(tiles), with_proj=proj is not None,
                          nffn=len(ffns)),
        grid=(sum(tiles),),
        in_specs=specs,
        out_specs=[stream_spec(s, d) for s in range(len(xs))],
        out_shape=[jax.ShapeDtypeStruct(x.shape, F32) for x in xs],
        scratch_shapes=[pltpu.VMEM((max(tms), d_ff), BF16)],
        compiler_params=_params(("arbitrary",)),
        name="ffn_proj" if proj is not None else "ffn",
    )(*args)


def _load_history(ext_ref, prev_ref, halo_rows):
    pb = prev_ref.shape[0]
    if halo_rows > pb:
        ext_ref[0:halo_rows - pb, :] = jnp.zeros((halo_rows - pb, ext_ref.shape[1]), F32)
    ext_ref[halo_rows - pb:halo_rows, :] = prev_ref[...]


def _pool_kernel(x_ref, prev_ref, g_ref, w_ref, sc_ref, o_ref, st_ref, ext_ref, *, stride, tt, halo, pos0, nsteps):
    t = pl.program_id(1)
    rows, hr = tt * stride, halo * stride
    pb = prev_ref.shape[0]

    @pl.when(t == 0)
    def _():
        _load_history(ext_ref, prev_ref, hr)

    x = x_ref[...]
    u = _rms(x, g_ref[...])
    ext_ref[hr:hr + rows, :] = u
    step = lax.broadcasted_iota(jnp.int32, (rows, 1), 0) >> int(math.log2(stride))
    pos = pos0 + t * tt + step
    gw = x.shape[1] // len(POOL_WINDOWS)
    for gi, win in enumerate(POOL_WINDOWS):
        cs = slice(gi * gw, (gi + 1) * gw)
        s = u[:, cs]
        for k in range(1, win):
            s = s + ext_ref[hr - k * stride:hr - k * stride + rows, cs]
        cnt = jnp.minimum(pos + 1, win).astype(F32)
        diff = (s / cnt - u[:, cs]).astype(BF16)
        o_ref[:, cs] = x[:, cs] + _dot(diff, w_ref[gi]) * sc_ref[:, cs]

    @pl.when(t == nsteps - 1)
    def _():
        st_ref[...] = ext_ref[hr + rows - pb:hr + rows, :]

    if nsteps > 1:
        ext_ref[0:hr, :] = ext_ref[rows:rows + hr, :]


def _seq_specs(rows, d):
    return pl.BlockSpec((None, rows, d), lambda g, t: (g, t, 0))


def _state_spec(shape):
    return pl.BlockSpec((None,) + tuple(shape[1:]), lambda g, t: (g,) + (0,) * (len(shape) - 1))


def _tiling(x, stride):
    groups, r, d = x.shape
    tt = min(SEQ_TILE, r) if stride == 1 else r // stride
    rows = tt * stride
    return groups, r, d, tt, rows, r // rows


def _pool_mixer(x, prev, g, w, scale, stride, pos0):
    groups, r, d, tt, rows, nsteps = _tiling(x, stride)
    halo = 16 if stride == 1 else POOL_BUF
    return pl.pallas_call(
        functools.partial(_pool_kernel, stride=stride, tt=tt, halo=halo, pos0=pos0, nsteps=nsteps),
        grid=(groups, nsteps),
        in_specs=[_seq_specs(rows, d), _state_spec(prev.shape), _const_spec(g.shape), _const_spec(w.shape),
                  _const_spec(scale.shape)],
        out_specs=[_seq_specs(rows, d), _state_spec(prev.shape)],
        out_shape=[jax.ShapeDtypeStruct(x.shape, F32), jax.ShapeDtypeStruct(prev.shape, F32)],
        scratch_shapes=[pltpu.VMEM((halo * stride + rows, d), F32)],
        compiler_params=_params(("arbitrary", "arbitrary")),
        name="pool_mixer",
    )(x, prev, g, w, scale)


def _rglru_kernel(x_ref, cprev_ref, hprev_ref, g_ref, wgate_ref, win_ref, cw_ref, cb_ref, wa_ref, ba_ref, wx_ref,
                  bx_ref, lam_ref, wout_ref, o_ref, cst_ref, hst_ref, ext_ref, h_ref, a_ref, b_ref,
                  *, stride, tt, halo, nsteps):
    t = pl.program_id(1)
    rows, hr = tt * stride, halo * stride
    pb = cprev_ref.shape[0]

    @pl.when(t == 0)
    def _():
        _load_history(ext_ref, cprev_ref, hr)
        h_ref[...] = hprev_ref[...]

    x = x_ref[...]
    u = _rms(x, g_ref[...]).astype(BF16)
    gate = jax.nn.gelu(_dot(u, wgate_ref[...]))
    zin = _dot(u, win_ref[...])
    ext_ref[hr:hr + rows, :] = zin
    z = cb_ref[...] + cw_ref[RG_CONV - 1:RG_CONV, :] * zin
    for k in range(RG_CONV - 1):
        back = (RG_CONV - 1 - k) * stride
        z = z + cw_ref[k:k + 1, :] * ext_ref[hr - back:hr - back + rows, :]
    zb = z.astype(BF16)
    lam = lam_ref[...]
    softplus_neg_lam = jnp.maximum(-lam, 0.0) + jnp.log1p(jnp.exp(-jnp.abs(lam)))
    nblk = wa_ref.shape[0]
    bw = x.shape[1] // nblk
    for n in range(nblk):
        cs = slice(n * bw, (n + 1) * bw)
        r = jax.nn.sigmoid(_dot(zb[:, cs], wa_ref[n]) + ba_ref[:, cs])
        i = jax.nn.sigmoid(_dot(zb[:, cs], wx_ref[n]) + bx_ref[:, cs])
        log_a = -RG_C * r * softplus_neg_lam[:, cs]
        a = jnp.exp(log_a)
        a_ref[:, cs] = a
        b_ref[:, cs] = jnp.sqrt(1.0 - a * a) * (i * z[:, cs])

    if stride == 1:
        def chunk(c, h):
            r0 = pl.multiple_of(c * SUBLANES, SUBLANES)
            a = a_ref[pl.ds(r0, SUBLANES), :]
            b = b_ref[pl.ds(r0, SUBLANES), :]
            row = lax.broadcasted_iota(jnp.int32, a.shape, 0)
            for s in (1, 2, 4):
                keep = row >= s
                b = jnp.where(keep, a * pltpu.roll(b, s, axis=0) + b, b)
                a = jnp.where(keep, a * pltpu.roll(a, s, axis=0), a)
            hs = a * h + b
            b_ref[pl.ds(r0, SUBLANES), :] = hs
            return hs[SUBLANES - 1:SUBLANES, :]

        h_ref[...] = lax.fori_loop(0, rows // SUBLANES, chunk, h_ref[...], unroll=4)
    else:
        h = h_ref[...]
        for s in range(tt):
            sl = slice(s * stride, (s + 1) * stride)
            h = a_ref[sl, :] * h + b_ref[sl, :]
            b_ref[sl, :] = h
        h_ref[...] = h

    y = _dot((gate * b_ref[...]).astype(BF16), wout_ref[...])
    o_ref[...] = x + y

    @pl.when(t == nsteps - 1)
    def _():
        cst_ref[...] = ext_ref[hr + rows - pb:hr + rows, :]
        hst_ref[...] = h_ref[...]

    if nsteps > 1:
        ext_ref[0:hr, :] = ext_ref[rows:rows + hr, :]


def _rglru_mixer(x, cprev, hprev, g, wgate, win, cw, cb, wa, ba, wx, bx, lam, wout, stride):
    groups, r, d, tt, rows, nsteps = _tiling(x, stride)
    halo = SUBLANES if stride == 1 else RG_CONV - 1
    consts = [g, wgate, win, cw, cb, wa, ba, wx, bx, lam, wout]
    return pl.pallas_call(
        functools.partial(_rglru_kernel, stride=stride, tt=tt, halo=halo, nsteps=nsteps),
        grid=(groups, nsteps),
        in_specs=[_seq_specs(rows, d), _state_spec(cprev.shape), _state_spec(hprev.shape)]
                 + [_const_spec(c.shape) for c in consts],
        out_specs=[_seq_specs(rows, d), _state_spec(cprev.shape), _state_spec(hprev.shape)],
        out_shape=[jax.ShapeDtypeStruct(x.shape, F32), jax.ShapeDtypeStruct(cprev.shape, F32),
                   jax.ShapeDtypeStruct(hprev.shape, F32)],
        scratch_shapes=[pltpu.VMEM((halo * stride + rows, d), F32), pltpu.VMEM((stride, d), F32),
                        pltpu.VMEM((rows, d), F32), pltpu.VMEM((rows, d), F32)],
        compiler_params=_params(("arbitrary", "arbitrary")),
        name="rglru_mixer",
    )(x, cprev, hprev, *consts)


def _convmod_kernel(x_ref, prev_ref, g_ref, w1_ref, b1_ref, dw_ref, dwb_ref, lng_ref, lnb_ref, w2_ref, b2_ref,
                    o_ref, st_ref, ext_ref, *, stride, tt, halo, nsteps):
    t = pl.program_id(1)
    rows, hr = tt * stride, halo * stride
    pb = prev_ref.shape[0]
    d = x_ref.shape[1]

    @pl.when(t == 0)
    def _():
        _load_history(ext_ref, prev_ref, hr)

    x = x_ref[...]
    u = _rms(x, g_ref[...]).astype(BF16)
    hc = _dot(u, w1_ref[...]) + b1_ref[...]
    glu = hc[:, :d] * jax.nn.sigmoid(hc[:, d:])
    ext_ref[hr:hr + rows, :] = glu
    c = dwb_ref[...]
    if stride == 1:
        for b in range(SUBLANES):
            phase = None
            for a in range((CV_WIDTH - 1 - b) // SUBLANES + 1):
                k = CV_WIDTH - 1 - (SUBLANES * a + b)
                start = hr - SUBLANES * (a + 1)
                term = dw_ref[k:k + 1, :] * ext_ref[start:start + rows + SUBLANES, :]
                phase = term if phase is None else phase + term
            c = c + phase[SUBLANES - b:SUBLANES - b + rows, :]
    else:
        for k in range(CV_WIDTH):
            back = (CV_WIDTH - 1 - k) * stride
            c = c + dw_ref[k:k + 1, :] * ext_ref[hr - back:hr - back + rows, :]
    cc = c - jnp.mean(c, axis=-1, keepdims=True)
    y = cc * lax.rsqrt(jnp.mean(cc * cc, axis=-1, keepdims=True) + NORM_EPS)
    y = y * lng_ref[...] + lnb_ref[...]
    y = (y * jax.nn.sigmoid(y)).astype(BF16)
    o_ref[...] = x + _dot(y, w2_ref[...]) + b2_ref[...]

    @pl.when(t == nsteps - 1)
    def _():
        st_ref[...] = ext_ref[hr + rows - pb:hr + rows, :]

    if nsteps > 1:
        ext_ref[0:hr, :] = ext_ref[rows:rows + hr, :]


def _conv_mixer(x, prev, g, w1, b1, dw, dwb, lng, lnb, w2, b2, stride):
    groups, r, d, tt, rows, nsteps = _tiling(x, stride)
    halo = 32 if stride == 1 else CV_WIDTH - 1
    consts = [g, w1, b1, dw, dwb, lng, lnb, w2, b2]
    return pl.pallas_call(
        functools.partial(_convmod_kernel, stride=stride, tt=tt, halo=halo, nsteps=nsteps),
        grid=(groups, nsteps),
        in_specs=[_seq_specs(rows, d), _state_spec(prev.shape)] + [_const_spec(c.shape) for c in consts],
        out_specs=[_seq_specs(rows, d), _state_spec(prev.shape)],
        out_shape=[jax.ShapeDtypeStruct(x.shape, F32), jax.ShapeDtypeStruct(prev.shape, F32)],
        scratch_shapes=[pltpu.VMEM((halo * stride + rows, d), F32)],
        compiler_params=_params(("arbitrary", "arbitrary")),
        name="conv_mixer",
    )(x, prev, g, w1, b1, dw, dwb, lng, lnb, w2, b2)


def _qkv_kernel(x_ref, g_ref, w_ref, qg_ref, kg_ref, cos_ref, sin_ref, seg_ref, q_ref, k_ref, v_ref, *flash_refs):
    d = x_ref.shape[1]
    u = _rms(x_ref[...], g_ref[...]).astype(BF16)
    qkv = _dot(u, w_ref[...])
    cos, sin = cos_ref[...], sin_ref[...]
    seg = seg_ref[...]
    lane = lax.broadcasted_iota(jnp.int32, (1, HEAD_WIDTH), 1)
    first_half = (lane & (HEAD_DIM - 1)) < HEAD_DIM // 2

    def norm_rope(blk, gain):
        sq = blk * blk
        hi = sq.astype(BF16)
        lo = (sq - hi.astype(F32)).astype(BF16)
        ms = _dot(hi, seg) + _dot(lo, seg)
        y = blk * lax.rsqrt(ms + NORM_EPS) * gain
        partner = jnp.where(first_half, pltpu.roll(y, HEAD_WIDTH - HEAD_DIM // 2, axis=1),
                            pltpu.roll(y, HEAD_DIM // 2, axis=1))
        return y * cos + partner * sin

    for h in range(d // HEAD_WIDTH):
        cs = slice(h * HEAD_WIDTH, (h + 1) * HEAD_WIDTH)
        q = norm_rope(qkv[:, cs], qg_ref[...])
        q_ref[:, cs] = (q * Q_SCALE).astype(BF16)
        k = norm_rope(qkv[:, d + h * HEAD_WIDTH:d + (h + 1) * HEAD_WIDTH], kg_ref[...])
        k_ref[:, cs] = k
        v = qkv[:, 2 * d + h * HEAD_WIDTH:2 * d + (h + 1) * HEAD_WIDTH]
        v_ref[:, cs] = v
        if flash_refs:
            kb_ref, vt_ref = flash_refs
            kb_ref[:, cs] = k.astype(BF16)
            vt_ref[h, 0:HEAD_WIDTH, :] = v.T.astype(BF16)
            pad = lax.broadcasted_iota(jnp.int32, (VT_ROWS - HEAD_WIDTH, v.shape[0]), 0)
            vt_ref[h, HEAD_WIDTH:VT_ROWS, :] = (pad == 0).astype(BF16)


def _qkv(x, g, w, qg, kg, cos, sin, seg, for_flash):
    n, d = x.shape
    tm = min(SEQ_TILE, n)
    nblk = cos.shape[0] // tm
    nh = d // HEAD_WIDTH
    row = pl.BlockSpec((tm, d), lambda i: (i, 0))
    tab = pl.BlockSpec((tm, HEAD_WIDTH), lambda i: (i % nblk, 0))
    out_specs = [row] * 3
    out_shape = [jax.ShapeDtypeStruct((n, d), dt) for dt in (BF16, F32, F32)]
    if for_flash:
        out_specs += [row, pl.BlockSpec((None, nh, None, VT_ROWS, tm), lambda i: (i // nblk, 0, i % nblk, 0, 0))]
        out_shape += [jax.ShapeDtypeStruct((n, d), BF16),
                      jax.ShapeDtypeStruct((n // (nblk * tm), nh, nblk, VT_ROWS, tm), BF16)]
    return pl.pallas_call(
        _qkv_kernel,
        grid=(n // tm,),
        in_specs=[row, _const_spec(g.shape), _const_spec(w.shape), _const_spec(qg.shape), _const_spec(kg.shape),
                  tab, tab, _const_spec(seg.shape)],
        out_specs=out_specs,
        out_shape=out_shape,
        compiler_params=_params(("arbitrary",)),
        name="qkv_rope",
    )(x, g, w, qg, kg, cos, sin, seg)


def _finish_heads(o, subln):
    ms = jnp.mean(o * o, axis=-1, keepdims=True)
    return o * lax.rsqrt(ms + NORM_EPS) * subln * (1.0 - LAMBDA_INIT)


def _flash_kernel(lam_ref, q_ref, k_ref, vt_ref, sub_ref, o_ref, acc_ref, sa_ref, sb_ref, m_ref, *, tile):
    i = pl.program_id(2)
    q = q_ref[...]
    lane = lax.broadcasted_iota(jnp.int32, q.shape, 1)
    qh = (jnp.where(lane < HEAD_DIM, q, jnp.zeros_like(q)), jnp.where(lane >= HEAD_DIM, q, jnp.zeros_like(q)))
    acc_ref[...] = jnp.zeros(acc_ref.shape, F32)
    m_ref[...] = jnp.full(m_ref.shape, NEG_INF, F32)

    def produce(j, s_ref, c):
        k = k_ref[pl.ds(pl.multiple_of(j * tile, tile), tile), :]
        s_ref[c] = _dot_nt(k, qh[c])

    def consume(j, s_ref, c, masked):
        st = s_ref[c]
        if masked:
            visible = (lax.broadcasted_iota(jnp.int32, (tile, tile), 0)
                       <= lax.broadcasted_iota(jnp.int32, (tile, tile), 1))
            st = jnp.where(visible, st, NEG_INF)
        m_old = m_ref[c]
        m_new = jnp.maximum(m_old, jnp.max(st, axis=0, keepdims=True))
        pt = jnp.exp2(st - m_new).astype(BF16)
        acc_ref[c] = jnp.exp2(m_old - m_new) * acc_ref[c] + _dot(vt_ref[j], pt)
        m_ref[c] = m_new

    def step(j, cur_ref, next_ref):
        for c in range(2):
            produce(j + 1, next_ref, c)
            consume(j, cur_ref, c, False)

    for c in range(2):
        produce(0, sa_ref, c)

    def body(j, carry):
        @pl.when(j % 2 == 0)
        def _():
            step(j, sa_ref, sb_ref)

        @pl.when(j % 2 == 1)
        def _():
            step(j, sb_ref, sa_ref)
        return carry

    lax.fori_loop(0, i, body, 0)

    @pl.when(i % 2 == 0)
    def _():
        for c in range(2):
            consume(i, sa_ref, c, True)

    @pl.when(i % 2 == 1)
    def _():
        for c in range(2):
            consume(i, sb_ref, c, True)

    a0, a1 = acc_ref[0], acc_ref[1]
    ot = (a0[:HEAD_WIDTH] / a0[HEAD_WIDTH:HEAD_WIDTH + 1]
          - lam_ref[...] * (a1[:HEAD_WIDTH] / a1[HEAD_WIDTH:HEAD_WIDTH + 1]))
    ms = jnp.mean(ot * ot, axis=0, keepdims=True)
    ot = ot * lax.rsqrt(ms + NORM_EPS) * sub_ref[...] * (1.0 - LAMBDA_INIT)
    o_ref[...] = ot.T.astype(BF16)


def _flash_attention(lam, q, k, vt, subln_col):
    b, t, d = q.shape
    nh, nblk, vrows, tile = vt.shape[1:]
    return pl.pallas_call(
        functools.partial(_flash_kernel, tile=tile),
        grid=(b, nh, nblk),
        in_specs=[pl.BlockSpec((1, 1), lambda bi, h, i: (0, 0)),
                  pl.BlockSpec((None, tile, HEAD_WIDTH), lambda bi, h, i: (bi, i, h)),
                  pl.BlockSpec((None, t, HEAD_WIDTH), lambda bi, h, i: (bi, 0, h)),
                  pl.BlockSpec((None, None, nblk, vrows, tile), lambda bi, h, i: (bi, h, 0, 0, 0)),
                  pl.BlockSpec((HEAD_WIDTH, 1), lambda bi, h, i: (0, 0))],
        out_specs=pl.BlockSpec((None, tile, HEAD_WIDTH), lambda bi, h, i: (bi, i, h)),
        out_shape=jax.ShapeDtypeStruct((b, t, d), BF16),
        scratch_shapes=[pltpu.VMEM((2, vrows, tile), F32), pltpu.VMEM((2, tile, tile), F32),
                        pltpu.VMEM((2, tile, tile), F32), pltpu.VMEM((2, 1, tile), F32)],
        compiler_params=_params(("arbitrary", "arbitrary", "arbitrary")),
        name="flash_diff_attention",
    )(lam, q, k, vt, subln_col)


def _lane_groups(x):
    return [x[:, g * HEAD_WIDTH:(g + 1) * HEAD_WIDTH] for g in range(x.shape[1] // HEAD_WIDTH)]


def _paged_kernel(pt_ref, lam_ref, qm_ref, knew_ref, vnew_ref, sub_ref, bias_ref, nbias_ref, *refs, npages, nq, nh):
    kp_refs, vp_refs = refs[:npages], refs[npages:2 * npages]
    o_ref, s_ref = refs[2 * npages:]
    qm = qm_ref[...]
    flat = lambda ref: ref[...].reshape(-1, HEAD_WIDTH).astype(BF16)

    kn = flat(knew_ref)
    s_new = _dot_nt(qm, kn) + nbias_ref[...]
    top = None
    for p in range(npages):
        s = _dot_nt(qm, flat(kp_refs[p])) + bias_ref[...]
        s_ref[p] = s
        page_top = functools.reduce(jnp.maximum, _lane_groups(s))
        top = page_top if top is None else jnp.maximum(top, page_top)
    m = jnp.maximum(jnp.max(top, axis=1, keepdims=True), jnp.max(s_new, axis=1, keepdims=True))
    m_lanes = jnp.broadcast_to(m, (m.shape[0], HEAD_WIDTH))

    p_new = jnp.exp2(s_new - m)
    acc = _dot(p_new.astype(BF16), flat(vnew_ref))
    lsum = None
    for p in range(npages):
        groups = [jnp.exp2(g - m_lanes) for g in _lane_groups(s_ref[p])]
        part = functools.reduce(jnp.add, groups)
        lsum = part if lsum is None else lsum + part
        acc = acc + _dot(jnp.concatenate(groups, axis=1).astype(BF16), flat(vp_refs[p]))
    l = jnp.sum(lsum, axis=1, keepdims=True) + jnp.sum(p_new, axis=1, keepdims=True)
    n = (acc / l).reshape(nh, 2 * nq, HEAD_WIDTH)
    o = n[:, 0:nq, :] - lam_ref[...] * n[:, nq:2 * nq, :]
    o_ref[...] = _finish_heads(o, sub_ref[...]).astype(BF16)


def _paged_attention(page_table, lam, qm, k_new, v_new, subln, cache_k, cache_v):
    bs, npages = page_table.shape
    _, nq, nh, _ = k_new.shape
    page_keys = cache_k.shape[1]
    nrow = qm.shape[1]
    page_bytes = page_keys * nh * HEAD_WIDTH * 4
    assert 2 * 2 * npages * page_bytes <= PAGED_VMEM_BUDGET, "all pages of a sequence must fit one grid step"

    row_head = jnp.arange(nrow)[:, None] // (2 * nq)
    row_query = jnp.arange(nrow)[:, None] % nq
    col = jnp.arange(page_keys * nh)[None, :]
    bias = jnp.where(col % nh == row_head, 0.0, NEG_INF).astype(F32)
    ncol = jnp.arange(nq * nh)[None, :]
    nbias = jnp.where((ncol % nh == row_head) & (ncol // nh <= row_query), 0.0, NEG_INF).astype(F32)

    page_block = (None,) + cache_k.shape[1:]
    page_spec = lambda p: pl.BlockSpec(page_block, lambda b, pt: (pt[b, p], 0, 0, 0))
    seq4 = lambda shape: pl.BlockSpec((None,) + shape[1:], lambda b, pt: (b, 0, 0, 0))
    const2 = lambda shape: pl.BlockSpec(shape, lambda b, pt: (0, 0))
    grid_spec = pltpu.PrefetchScalarGridSpec(
        num_scalar_prefetch=1,
        grid=(bs,),
        in_specs=[const2((1, 1)),
                  pl.BlockSpec((None,) + qm.shape[1:], lambda b, pt: (b, 0, 0)),
                  seq4(k_new.shape), seq4(v_new.shape), const2((1, HEAD_WIDTH)), const2(bias.shape),
                  const2(nbias.shape)]
                 + [page_spec(p) for p in range(npages)] * 2,
        out_specs=pl.BlockSpec((None, nh, nq, HEAD_WIDTH), lambda b, pt: (b, 0, 0, 0)),
        scratch_shapes=[pltpu.VMEM((npages, nrow, page_keys * nh), F32)],
    )
    return pl.pallas_call(
        functools.partial(_paged_kernel, npages=npages, nq=nq, nh=nh),
        grid_spec=grid_spec,
        out_shape=jax.ShapeDtypeStruct((bs, nh, nq, HEAD_WIDTH), BF16),
        compiler_params=_params(("arbitrary",)),
        name="paged_diff_attention",
    )(page_table, lam, qm, k_new, v_new, subln, bias, nbias, *([cache_k] * npages), *([cache_v] * npages))


def _lam_kernel(q1_ref, k1_ref, q2_ref, k2_ref, o_ref):
    dot1 = jnp.sum(q1_ref[...] * k1_ref[...], axis=-1, keepdims=True)
    dot2 = jnp.sum(q2_ref[...] * k2_ref[...], axis=-1, keepdims=True)
    o_ref[...] = jnp.exp(dot1) - jnp.exp(dot2) + LAMBDA_INIT


def _rope_tables(pos):
    half = HEAD_DIM // 2
    inv = ROPE_THETA ** (-jnp.arange(half, dtype=F32) / half)
    ang = pos.astype(F32)[:, None] * inv[None, :]
    cos, sin = jnp.cos(ang), jnp.sin(ang)
    return jnp.tile(cos, (1, 4)), jnp.tile(jnp.concatenate([-sin, sin], axis=1), (1, 2))


def _mixer(i, x, stride, pos0, st, attend, p):
    groups, r, d = x.shape
    g = p['mix_norm'][i]
    if i == 0:
        x, pool = _pool_mixer(x, st['pool'], g, p['pool_w'], p['pool_scale'], stride, pos0)
        return x, dict(pool=pool), None
    if i == 1:
        x, rgc, rgh = _rglru_mixer(
            x, st['rgc'], st['rgh'], g, p['rg_w_gate'], p['rg_w_in'], p['rg_conv_w'], p['rg_conv_b'], p['rg_wa'],
            p['rg_ba'], p['rg_wx'], p['rg_bx'], p['rg_lambda'], p['rg_w_out'], stride)
        return x, dict(rgc=rgc, rgh=rgh), None
    if i == 2:
        x, cv = _conv_mixer(x, st['cv'], g, p['cv_w_pw1'], p['cv_b_pw1'], p['cv_dw_w'], p['cv_dw_b'],
                            p['cv_ln_g'], p['cv_ln_b'], p['cv_w_pw2'], p['cv_b_pw2'], stride)
        return x, dict(cv=cv), None
    pos = pos0 + jnp.arange(r, dtype=jnp.int32) // stride
    cos, sin = _rope_tables(pos)
    q, k, v, *flash_kv = _qkv(x.reshape(groups * r, d), g, p['at_w_qkv'], p['at_q_norm'], p['at_k_norm'], cos, sin,
                              p['seg'], for_flash=stride == 1)
    return x, dict(k=k, v=v), attend(q, k, v, *flash_kv)


def _trunk(groups, p):
    xs = [grp['x'] for grp in groups]
    outs = [dict() for _ in groups]
    nlayers = p['mix_norm'].shape[0]

    def ffn_weights(which, i):
        return tuple(p[f'{which}_{name}'][i] for name in ('norm', 'w_gate', 'w_up', 'w_down'))

    def run_ffn(xs, ffns, proj=None):
        ys = _ffn([x.reshape(-1, x.shape[-1]) for x in xs], ffns, proj=proj)
        return [y.reshape(x.shape) for x, y in zip(xs, ys)]

    xs = run_ffn(xs, [ffn_weights('ffn1', 0)])
    for i in range(nlayers):
        heads = []
        for n, grp in enumerate(groups):
            xs[n], new_state, pending = _mixer(i, xs[n], grp['stride'], grp['pos0'], grp['state'], grp['attend'], p)
            outs[n].update(new_state)
            heads.append(pending)
        proj = (heads, p['at_w_o']) if heads[0] is not None else None
        ffns = [ffn_weights('ffn2', i)] + ([ffn_weights('ffn1', i + 1)] if i + 1 < nlayers else [])
        xs = run_ffn(xs, ffns, proj=proj)
    return xs, outs


def kernel(x_prompt, x_sample, state_pool, state_rglru_conv, state_rglru_h, state_conv, cache_k, cache_v, page_table, ffn1_norm, ffn1_w_gate, ffn1_w_up, ffn1_w_down, mix_norm, ffn2_norm, ffn2_w_gate, ffn2_w_up, ffn2_w_down, pool_w, pool_scale, rg_w_gate, rg_w_in, rg_conv_w, rg_conv_b, rg_wa, rg_ba, rg_wx, rg_bx, rg_lambda, rg_w_out, cv_w_pw1, cv_b_pw1, cv_dw_w, cv_dw_b, cv_ln_g, cv_ln_b, cv_w_pw2, cv_b_pw2, at_w_qkv, at_q_norm, at_k_norm, at_lam_q1, at_lam_k1, at_lam_q2, at_lam_k2, at_subln, at_w_o):
    bp, tp, d = x_prompt.shape
    bs, ts, _ = x_sample.shape
    nh = d // HEAD_WIDTH
    past = page_table.shape[1] * cache_k.shape[1]
    bf = lambda w: w.astype(BF16)
    bfl = lambda w: [w[i].astype(BF16) for i in range(w.shape[0])]
    row = lambda v: v.reshape(1, -1).astype(F32)
    rows = lambda v: v.reshape(v.shape[0], 1, -1).astype(F32)

    seg_id = jnp.arange(HEAD_WIDTH) // HEAD_DIM
    p = dict(
        ffn1_norm=rows(ffn1_norm), ffn1_w_gate=bfl(ffn1_w_gate), ffn1_w_up=bfl(ffn1_w_up),
        ffn1_w_down=bfl(ffn1_w_down),
        mix_norm=rows(mix_norm),
        ffn2_norm=rows(ffn2_norm), ffn2_w_gate=bfl(ffn2_w_gate), ffn2_w_up=bfl(ffn2_w_up),
        ffn2_w_down=bfl(ffn2_w_down),
        pool_w=bf(pool_w), pool_scale=row(pool_scale),
        rg_w_gate=bf(rg_w_gate), rg_w_in=bf(rg_w_in), rg_conv_w=rg_conv_w, rg_conv_b=row(rg_conv_b),
        rg_wa=bf(rg_wa), rg_ba=row(rg_ba), rg_wx=bf(rg_wx), rg_bx=row(rg_bx), rg_lambda=row(rg_lambda),
        rg_w_out=bf(rg_w_out),
        cv_w_pw1=bf(cv_w_pw1), cv_b_pw1=row(cv_b_pw1), cv_dw_w=cv_dw_w, cv_dw_b=row(cv_dw_b), cv_ln_g=row(cv_ln_g),
        cv_ln_b=row(cv_ln_b), cv_w_pw2=bf(cv_w_pw2), cv_b_pw2=row(cv_b_pw2),
        at_w_qkv=bf(at_w_qkv), at_q_norm=row(jnp.tile(at_q_norm, 2)), at_k_norm=row(jnp.tile(at_k_norm, 2)),
        at_w_o=bf(at_w_o),
        seg=((seg_id[:, None] == seg_id[None, :]).astype(F32) / HEAD_DIM).astype(BF16),
    )
    subln = row(at_subln)
    lam = pl.pallas_call(
        _lam_kernel,
        in_specs=[_const_spec((1, HEAD_DIM))] * 4,
        out_specs=_const_spec((1, 1)),
        out_shape=jax.ShapeDtypeStruct((1, 1), F32),
        name="diff_lambda",
    )(row(at_lam_q1), row(at_lam_k1), row(at_lam_q2), row(at_lam_k2))

    def attend_prompt(q, k, v, kb, vt):
        shape = (bp, tp, d)
        return _flash_attention(lam, q.reshape(shape), kb.reshape(shape), vt, subln.reshape(-1, 1)).reshape(-1, d)

    zeros = lambda n: jnp.zeros((bp, n, d), F32)
    prompt = dict(x=x_prompt, stride=1, pos0=0, attend=attend_prompt,
                  state=dict(pool=zeros(POOL_BUF), rgc=zeros(RG_CONV - 1), rgh=zeros(1), cv=zeros(CV_WIDTH - 1)))

    def to_tm(a):
        return jnp.swapaxes(a, 0, 1).reshape(1, a.shape[1] * bs, d)

    def from_tm(a):
        return jnp.swapaxes(a.reshape(-1, bs, d), 0, 1)

    def attend_sample(q, k, v):
        q5 = from_tm(q).reshape(bs, ts, nh, 2, HEAD_DIM)
        eye = jnp.eye(2, dtype=q.dtype)
        qm = jnp.einsum('bqhcd,ce->bhcqed', q5, eye).reshape(bs, nh * 2 * ts, HEAD_WIDTH)
        k_new = from_tm(k).reshape(bs, ts, nh, HEAD_WIDTH)
        v_new = from_tm(v).reshape(bs, ts, nh, HEAD_WIDTH)
        o = _paged_attention(page_table, lam, qm, k_new, v_new, subln, cache_k, cache_v)
        return jnp.transpose(o, (2, 0, 1, 3)).reshape(ts * bs, d)

    sample = dict(x=to_tm(x_sample), stride=bs, pos0=past, attend=attend_sample,
                  state=dict(pool=to_tm(state_pool), rgc=to_tm(state_rglru_conv),
                             rgh=state_rglru_h.reshape(1, bs, d), cv=to_tm(state_conv)))
    (y_p, y_s), (o_p, o_s) = _trunk([prompt, sample], p)

    kv4 = lambda a, b, t: a.reshape(b, t, nh, HEAD_WIDTH)
    return (y_p, from_tm(y_s),
            o_p['pool'], from_tm(o_s['pool']),
            o_p['rgc'], from_tm(o_s['rgc']),
            o_p['rgh'].reshape(bp, d), o_s['rgh'].reshape(bs, d),
            o_p['cv'], from_tm(o_s['cv']),
            kv4(o_p['k'], bp, tp), kv4(o_p['v'], bp, tp),
            kv4(from_tm(o_s['k']), bs, ts), kv4(from_tm(o_s['v']), bs, ts))
```

```python
import functools
import math

import jax
import jax.numpy as jnp
from jax import lax
from jax.experimental import pallas as pl
from jax.experimental.pallas import tpu as pltpu

F32 = jnp.float32
BF16 = jnp.bfloat16

NORM_EPS = 1e-6
POOL_WINDOWS = (2, 4, 8, 16)
POOL_BUF = max(POOL_WINDOWS) - 1
RG_CONV = 4
RG_C = 8.0
CV_WIDTH = 31
HEAD_DIM = 64
HEAD_WIDTH = 2 * HEAD_DIM
ROPE_THETA = 10000.0
ATT_LAYER = 3
LAMBDA_INIT = 0.8 - 0.6 * math.exp(-0.3 * ATT_LAYER)
NEG_INF = -1e30

Q_SCALE = HEAD_DIM ** -0.5 * math.log2(math.e)

SUBLANES = 8
VMEM_LIMIT = 56 * 1024 * 1024
PAGED_VMEM_BUDGET = 36 * 1024 * 1024
SEQ_TILE = 512
FFN_CHUNK = 256
FLASH_UNROLL = 4
VT_ROWS = HEAD_WIDTH + 16


def _params(sem):
    return pltpu.CompilerParams(dimension_semantics=sem, vmem_limit_bytes=VMEM_LIMIT)


def _rms(x, g):
    ms = jnp.mean(x * x, axis=-1, keepdims=True)
    return x * lax.rsqrt(ms + NORM_EPS) * g


def _dot(a, b):
    return jnp.dot(a, b, preferred_element_type=F32)


def _dot_nt(a, b):
    return lax.dot_general(a, b, (((1,), (1,)), ((), ())), preferred_element_type=F32)


def _const_spec(shape):
    n = len(shape)
    return pl.BlockSpec(shape, lambda *_: (0,) * n)


def _ffn_kernel(*refs, nstreams, tiles, with_proj, nffn):
    per = 2 if with_proj else 1
    ins = refs[:nstreams * per]
    rest = refs[nstreams * per:]
    if with_proj:
        wo_ref, rest = rest[0], rest[1:]
    ffns = [rest[4 * n:4 * n + 4] for n in range(nffn)]
    outs = rest[4 * nffn:4 * nffn + nstreams]
    h_ref = rest[4 * nffn + nstreams]
    i = pl.program_id(0)
    start = 0
    for s in range(nstreams):
        @pl.when((i >= start) & (i < start + tiles[s]))
        def _(s=s):
            x = ins[s * per][...]
            if with_proj:
                x = x + _dot(ins[s * per + 1][...], wo_ref[...])
            rows = x.shape[0]
            for g_ref, wg_ref, wu_ref, wd_ref in ffns:
                u = _rms(x, g_ref[...]).astype(BF16)
                for c in range(wg_ref.shape[1] // FFN_CHUNK):
                    sl = slice(c * FFN_CHUNK, (c + 1) * FFN_CHUNK)
                    gate = _dot(u, wg_ref[:, sl])
                    up = _dot(u, wu_ref[:, sl])
                    h_ref[0:rows, sl] = (gate * jax.nn.sigmoid(gate) * up).astype(BF16)
                x = x + 0.5 * _dot(h_ref[0:rows, :], wd_ref[...])
            outs[s][...] = x
        start += tiles[s]


def _ffn(xs, ffns, proj=None):
    d = xs[0].shape[1]
    d_ff = ffns[0][2].shape[2]
    tms = [min(SEQ_TILE, x.shape[0]) for x in xs]
    tiles = [x.shape[0] // tm for x, tm in zip(xs, tms)]
    offsets = [sum(tiles[:s]) for s in range(len(xs))]
    single = pl.Buffered(1)

    def stream_spec(s, width):
        return pl.BlockSpec((tms[s], width),
                            lambda i, s=s: (jnp.clip(i - offsets[s], 0, tiles[s] - 1), 0))

    args, specs = [], []
    for s, x in enumerate(xs):
        args.append(x)
        specs.append(stream_spec(s, d))
        if proj is not None:
            args.append(proj[0][s])
            specs.append(stream_spec(s, proj[0][s].shape[1]))
    if proj is not None:
        args.append(proj[1])
        specs.append(pl.BlockSpec(proj[1].shape, lambda i: (0, 0), pipeline_mode=single))
    for layer, g, wg, wu, wd in ffns:
        pick = lambda i, layer=layer: (layer, 0, 0)
        args += [g, wg, wu, wd]
        specs += [pl.BlockSpec((None, 1, d), pick),
                  pl.BlockSpec((None, d, d_ff), pick, pipeline_mode=single),
                  pl.BlockSpec((None, d, d_ff), pick, pipeline_mode=single),
                  pl.BlockSpec((None, d_ff, d), pick, pipeline_mode=single)]
    return pl.pallas_call(
        functools.partial(_ffn_kernel, nstreams=len(xs), tiles=tuple(tiles), with_proj=proj is not None,
                          nffn=len(ffns)),
        grid=(sum(tiles),),
        in_specs=specs,
        out_specs=[stream_spec(s, d) for s in range(len(xs))],
        out_shape=[jax.ShapeDtypeStruct(x.shape, F32) for x in xs],
        scratch_shapes=[pltpu.VMEM((max(tms), d_ff), BF16)],
        compiler_params=_params(("arbitrary",)),
        name="ffn_proj" if proj is not None else "ffn",
    )(*args)


def _load_history(ext_ref, prev_ref, halo_rows):
    pb = prev_ref.shape[0]
    if halo_rows > pb:
        ext_ref[0:halo_rows - pb, :] = jnp.zeros((halo_rows - pb, ext_ref.shape[1]), F32)
    ext_ref[halo_rows - pb:halo_rows, :] = prev_ref[...]


def _pool_kernel(x_ref, prev_ref, g_ref, w_ref, sc_ref, o_ref, st_ref, ext_ref, *, stride, tt, halo, pos0, nsteps):
    t = pl.program_id(1)
    rows, hr = tt * stride, halo * stride
    pb = prev_ref.shape[0]

    @pl.when(t == 0)
    def _():
        _load_history(ext_ref, prev_ref, hr)

    x = x_ref[...]
    u = _rms(x, g_ref[...])
    ext_ref[hr:hr + rows, :] = u
    step = lax.broadcasted_iota(jnp.int32, (rows, 1), 0) >> int(math.log2(stride))
    pos = pos0 + t * tt + step
    gw = x.shape[1] // len(POOL_WINDOWS)
    for gi, win in enumerate(POOL_WINDOWS):
        cs = slice(gi * gw, (gi + 1) * gw)
        s = u[:, cs]
        for k in range(1, win):
            s = s + ext_ref[hr - k * stride:hr - k * stride + rows, cs]
        cnt = jnp.minimum(pos + 1, win).astype(F32)
        diff = (s / cnt - u[:, cs]).astype(BF16)
        o_ref[:, cs] = x[:, cs] + _dot(diff, w_ref[gi]) * sc_ref[:, cs]

    @pl.when(t == nsteps - 1)
    def _():
        st_ref[...] = ext_ref[hr + rows - pb:hr + rows, :]

    if nsteps > 1:
        ext_ref[0:hr, :] = ext_ref[rows:rows + hr, :]


def _seq_specs(rows, d):
    return pl.BlockSpec((None, rows, d), lambda g, t: (g, t, 0))


def _state_spec(shape):
    return pl.BlockSpec((None,) + tuple(shape[1:]), lambda g, t: (g,) + (0,) * (len(shape) - 1))


def _tiling(x, stride):
    groups, r, d = x.shape
    tt = min(SEQ_TILE, r) if stride == 1 else r // stride
    rows = tt * stride
    return groups, r, d, tt, rows, r // rows


def _pool_mixer(x, prev, g, w, scale, stride, pos0):
    groups, r, d, tt, rows, nsteps = _tiling(x, stride)
    halo = 16 if stride == 1 else POOL_BUF
    return pl.pallas_call(
        functools.partial(_pool_kernel, stride=stride, tt=tt, halo=halo, pos0=pos0, nsteps=nsteps),
        grid=(groups, nsteps),
        in_specs=[_seq_specs(rows, d), _state_spec(prev.shape), _const_spec(g.shape), _const_spec(w.shape),
                  _const_spec(scale.shape)],
        out_specs=[_seq_specs(rows, d), _state_spec(prev.shape)],
        out_shape=[jax.ShapeDtypeStruct(x.shape, F32), jax.ShapeDtypeStruct(prev.shape, F32)],
        scratch_shapes=[pltpu.VMEM((halo * stride + rows, d), F32)],
        compiler_params=_params(("arbitrary", "arbitrary")),
        name="pool_mixer",
    )(x, prev, g, w, scale)


def _rglru_kernel(x_ref, cprev_ref, hprev_ref, g_ref, wgate_ref, win_ref, cw_ref, cb_ref, wa_ref, ba_ref, wx_ref,
                  bx_ref, lam_ref, wout_ref, o_ref, cst_ref, hst_ref, ext_ref, h_ref, a_ref, b_ref,
                  *, stride, tt, halo, nsteps):
    t = pl.program_id(1)
    rows, hr = tt * stride, halo * stride
    pb = cprev_ref.shape[0]

    @pl.when(t == 0)
    def _():
        _load_history(ext_ref, cprev_ref, hr)
        h_ref[...] = hprev_ref[...]

    x = x_ref[...]
    u = _rms(x, g_ref[...]).astype(BF16)
    gate = jax.nn.gelu(_dot(u, wgate_ref[...]))
    zin = _dot(u, win_ref[...])
    ext_ref[hr:hr + rows, :] = zin
    z = cb_ref[...] + cw_ref[RG_CONV - 1:RG_CONV, :] * zin
    for k in range(RG_CONV - 1):
        back = (RG_CONV - 1 - k) * stride
        z = z + cw_ref[k:k + 1, :] * ext_ref[hr - back:hr - back + rows, :]
    zb = z.astype(BF16)
    lam = lam_ref[...]
    softplus_neg_lam = jnp.maximum(-lam, 0.0) + jnp.log1p(jnp.exp(-jnp.abs(lam)))
    nblk = wa_ref.shape[0]
    bw = x.shape[1] // nblk
    for n in range(nblk):
        cs = slice(n * bw, (n + 1) * bw)
        r = jax.nn.sigmoid(_dot(zb[:, cs], wa_ref[n]) + ba_ref[:, cs])
        i = jax.nn.sigmoid(_dot(zb[:, cs], wx_ref[n]) + bx_ref[:, cs])
        log_a = -RG_C * r * softplus_neg_lam[:, cs]
        a = jnp.exp(log_a)
        a_ref[:, cs] = a
        b_ref[:, cs] = jnp.sqrt(1.0 - a * a) * (i * z[:, cs])

    if stride == 1:
        def chunk(c, h):
            r0 = pl.multiple_of(c * SUBLANES, SUBLANES)
            a = a_ref[pl.ds(r0, SUBLANES), :]
            b = b_ref[pl.ds(r0, SUBLANES), :]
            row = lax.broadcasted_iota(jnp.int32, a.shape, 0)
            for s in (1, 2, 4):
                keep = row >= s
                b = jnp.where(keep, a * pltpu.roll(b, s, axis=0) + b, b)
                a = jnp.where(keep, a * pltpu.roll(a, s, axis=0), a)
            hs = a * h + b
            b_ref[pl.ds(r0, SUBLANES), :] = hs
            return hs[SUBLANES - 1:SUBLANES, :]

        h_ref[...] = lax.fori_loop(0, rows // SUBLANES, chunk, h_ref[...], unroll=4)
    else:
        h = h_ref[...]
        for s in range(tt):
            sl = slice(s * stride, (s + 1) * stride)
            h = a_ref[sl, :] * h + b_ref[sl, :]
            b_ref[sl, :] = h
        h_ref[...] = h

    y = _dot((gate * b_ref[...]).astype(BF16), wout_ref[...])
    o_ref[...] = x + y

    @pl.when(t == nsteps - 1)
    def _():
        cst_ref[...] = ext_ref[hr + rows - pb:hr + rows, :]
        hst_ref[...] = h_ref[...]

    if nsteps > 1:
        ext_ref[0:hr, :] = ext_ref[rows:rows + hr, :]


def _rglru_mixer(x, cprev, hprev, g, wgate, win, cw, cb, wa, ba, wx, bx, lam, wout, stride):
    groups, r, d, tt, rows, nsteps = _tiling(x, stride)
    halo = SUBLANES if stride == 1 else RG_CONV - 1
    consts = [g, wgate, win, cw, cb, wa, ba, wx, bx, lam, wout]
    return pl.pallas_call(
        functools.partial(_rglru_kernel, stride=stride, tt=tt, halo=halo, nsteps=nsteps),
        grid=(groups, nsteps),
        in_specs=[_seq_specs(rows, d), _state_spec(cprev.shape), _state_spec(hprev.shape)]
                 + [_const_spec(c.shape) for c in consts],
        out_specs=[_seq_specs(rows, d), _state_spec(cprev.shape), _state_spec(hprev.shape)],
        out_shape=[jax.ShapeDtypeStruct(x.shape, F32), jax.ShapeDtypeStruct(cprev.shape, F32),
                   jax.ShapeDtypeStruct(hprev.shape, F32)],
        scratch_shapes=[pltpu.VMEM((halo * stride + rows, d), F32), pltpu.VMEM((stride, d), F32),
                        pltpu.VMEM((rows, d), F32), pltpu.VMEM((rows, d), F32)],
        compiler_params=_params(("arbitrary", "arbitrary")),
        name="rglru_mixer",
    )(x, cprev, hprev, *consts)


def _convmod_kernel(x_ref, prev_ref, g_ref, w1_ref, b1_ref, dw_ref, dwb_ref, lng_ref, lnb_ref, w2_ref, b2_ref,
                    o_ref, st_ref, ext_ref, *, stride, tt, halo, nsteps):
    t = pl.program_id(1)
    rows, hr = tt * stride, halo * stride
    pb = prev_ref.shape[0]
    d = x_ref.shape[1]

    @pl.when(t == 0)
    def _():
        _load_history(ext_ref, prev_ref, hr)

    x = x_ref[...]
    u = _rms(x, g_ref[...]).astype(BF16)
    hc = _dot(u, w1_ref[...]) + b1_ref[...]
    glu = hc[:, :d] * jax.nn.sigmoid(hc[:, d:])
    ext_ref[hr:hr + rows, :] = glu
    c = dwb_ref[...]
    if stride == 1:
        for b in range(SUBLANES):
            phase = None
            for a in range((CV_WIDTH - 1 - b) // SUBLANES + 1):
                k = CV_WIDTH - 1 - (SUBLANES * a + b)
                start = hr - SUBLANES * (a + 1)
                term = dw_ref[k:k + 1, :] * ext_ref[start:start + rows + SUBLANES, :]
                phase = term if phase is None else phase + term
            c = c + phase[SUBLANES - b:SUBLANES - b + rows, :]
    else:
        for k in range(CV_WIDTH):
            back = (CV_WIDTH - 1 - k) * stride
            c = c + dw_ref[k:k + 1, :] * ext_ref[hr - back:hr - back + rows, :]
    cc = c - jnp.mean(c, axis=-1, keepdims=True)
    y = cc * lax.rsqrt(jnp.mean(cc * cc, axis=-1, keepdims=True) + NORM_EPS)
    y = y * lng_ref[...] + lnb_ref[...]
    y = (y * jax.nn.sigmoid(y)).astype(BF16)
    o_ref[...] = x + _dot(y, w2_ref[...]) + b2_ref[...]

    @pl.when(t == nsteps - 1)
    def _():
        st_ref[...] = ext_ref[hr + rows - pb:hr + rows, :]

    if nsteps > 1:
        ext_ref[0:hr, :] = ext_ref[rows:rows + hr, :]


def _conv_mixer(x, prev, g, w1, b1, dw, dwb, lng, lnb, w2, b2, stride):
    groups, r, d, tt, rows, nsteps = _tiling(x, stride)
    halo = 32 if stride == 1 else CV_WIDTH - 1
    consts = [g, w1, b1, dw, dwb, lng, lnb, w2, b2]
    return pl.pallas_call(
        functools.partial(_convmod_kernel, stride=stride, tt=tt, halo=halo, nsteps=nsteps),
        grid=(groups, nsteps),
        in_specs=[_seq_specs(rows, d), _state_spec(prev.shape)] + [_const_spec(c.shape) for c in consts],
        out_specs=[_seq_specs(rows, d), _state_spec(prev.shape)],
        out_shape=[jax.ShapeDtypeStruct(x.shape, F32), jax.ShapeDtypeStruct(prev.shape, F32)],
        scratch_shapes=[pltpu.VMEM((halo * stride + rows, d), F32)],
        compiler_params=_params(("arbitrary", "arbitrary")),
        name="conv_mixer",
    )(x, prev, g, w1, b1, dw, dwb, lng, lnb, w2, b2)


def _qkv_kernel(x_ref, g_ref, w_ref, qg_ref, kg_ref, cos_ref, sin_ref, seg_ref, q_ref, k_ref, v_ref, *flash_refs):
    d = x_ref.shape[1]
    u = _rms(x_ref[...], g_ref[...]).astype(BF16)
    qkv = _dot(u, w_ref[...])
    cos, sin = cos_ref[...], sin_ref[...]
    seg = seg_ref[...]
    lane = lax.broadcasted_iota(jnp.int32, (1, HEAD_WIDTH), 1)
    first_half = (lane & (HEAD_DIM - 1)) < HEAD_DIM // 2

    def norm_rope(blk, gain):
        sq = blk * blk
        hi = sq.astype(BF16)
        lo = (sq - hi.astype(F32)).astype(BF16)
        ms = _dot(hi, seg) + _dot(lo, seg)
        y = blk * lax.rsqrt(ms + NORM_EPS) * gain
        partner = jnp.where(first_half, pltpu.roll(y, HEAD_WIDTH - HEAD_DIM // 2, axis=1),
                            pltpu.roll(y, HEAD_DIM // 2, axis=1))
        return y * cos + partner * sin

    for h in range(d // HEAD_WIDTH):
        cs = slice(h * HEAD_WIDTH, (h + 1) * HEAD_WIDTH)
        q = norm_rope(qkv[:, cs], qg_ref[...])
        q_ref[:, cs] = (q * Q_SCALE).astype(BF16)
        k = norm_rope(qkv[:, d + h * HEAD_WIDTH:d + (h + 1) * HEAD_WIDTH], kg_ref[...])
        k_ref[:, cs] = k
        v = qkv[:, 2 * d + h * HEAD_WIDTH:2 * d + (h + 1) * HEAD_WIDTH]
        v_ref[:, cs] = v
        if flash_refs:
            kb_ref, vt_ref = flash_refs
            kb_ref[:, cs] = k.astype(BF16)
            vt_ref[h, 0:HEAD_WIDTH, :] = v.T.astype(BF16)
            pad = lax.broadcasted_iota(jnp.int32, (VT_ROWS - HEAD_WIDTH, v.shape[0]), 0)
            vt_ref[h, HEAD_WIDTH:VT_ROWS, :] = (pad == 0).astype(BF16)


def _qkv(x, g, w, qg, kg, cos, sin, seg, for_flash):
    n, d = x.shape
    tm = min(SEQ_TILE, n)
    nblk = cos.shape[0] // tm
    nh = d // HEAD_WIDTH
    row = pl.BlockSpec((tm, d), lambda i: (i, 0))
    tab = pl.BlockSpec((tm, HEAD_WIDTH), lambda i: (i % nblk, 0))
    out_specs = [row] * 3
    out_shape = [jax.ShapeDtypeStruct((n, d), dt) for dt in (BF16, F32, F32)]
    if for_flash:
        out_specs += [row, pl.BlockSpec((None, nh, None, VT_ROWS, tm), lambda i: (i // nblk, 0, i % nblk, 0, 0))]
        out_shape += [jax.ShapeDtypeStruct((n, d), BF16),
                      jax.ShapeDtypeStruct((n // (nblk * tm), nh, nblk, VT_ROWS, tm), BF16)]
    return pl.pallas_call(
        _qkv_kernel,
        grid=(n // tm,),
        in_specs=[row, _const_spec(g.shape), _const_spec(w.shape), _const_spec(qg.shape), _const_spec(kg.shape),
                  tab, tab, _const_spec(seg.shape)],
        out_specs=out_specs,
        out_shape=out_shape,
        compiler_params=_params(("arbitrary",)),
        name="qkv_rope",
    )(x, g, w, qg, kg, cos, sin, seg)


def _finish_heads(o, subln):
    ms = jnp.mean(o * o, axis=-1, keepdims=True)
    return o * lax.rsqrt(ms + NORM_EPS) * subln * (1.0 - LAMBDA_INIT)


def _flash_kernel(lam_ref, q_ref, k_ref, vt_ref, sub_ref, o_ref, acc_ref, sa_ref, sb_ref, m_ref, *, tile):
    i = pl.program_id(2)
    q = q_ref[...]
    lane = lax.broadcasted_iota(jnp.int32, q.shape, 1)
    qh = (jnp.where(lane < HEAD_DIM, q, jnp.zeros_like(q)), jnp.where(lane >= HEAD_DIM, q, jnp.zeros_like(q)))
    acc_ref[...] = jnp.zeros(acc_ref.shape, F32)
    m_ref[...] = jnp.full(m_ref.shape, NEG_INF, F32)

    def produce(j, s_ref, c):
        k = k_ref[pl.ds(pl.multiple_of(j * tile, tile), tile), :]
        s_ref[c] = _dot_nt(k, qh[c])

    def consume(j, s_ref, c, masked):
        st = s_ref[c]
        if masked:
            visible = (lax.broadcasted_iota(jnp.int32, (tile, tile), 0)
                       <= lax.broadcasted_iota(jnp.int32, (tile, tile), 1))
            st = jnp.where(visible, st, NEG_INF)
        m_old = m_ref[c]
        m_new = jnp.maximum(m_old, jnp.max(st, axis=0, keepdims=True))
        pt = jnp.exp2(st - m_new).astype(BF16)
        acc_ref[c] = jnp.exp2(m_old - m_new) * acc_ref[c] + _dot(vt_ref[j], pt)
        m_ref[c] = m_new

    bufs = (sa_ref, sb_ref)

    def steps(j0, n, slot):
        for u in range(n):
            for c in range(2):
                produce(j0 + u + 1, bufs[(slot + u + 1) % 2], c)
                consume(j0 + u, bufs[(slot + u) % 2], c, False)

    for c in range(2):
        produce(0, sa_ref, c)

    def group(g, carry):
        steps(g * FLASH_UNROLL, FLASH_UNROLL, 0)
        return carry

    lax.fori_loop(0, i // FLASH_UNROLL, group, 0)
    for r in range(FLASH_UNROLL):
        @pl.when(i % FLASH_UNROLL == r)
        def _(r=r):
            steps(i - r, r, 0)
            for c in range(2):
                consume(i, bufs[r % 2], c, True)

    a0, a1 = acc_ref[0], acc_ref[1]
    ot = (a0[:HEAD_WIDTH] * (1.0 / a0[HEAD_WIDTH:HEAD_WIDTH + 1])
          - a1[:HEAD_WIDTH] * (lam_ref[...] / a1[HEAD_WIDTH:HEAD_WIDTH + 1]))
    ms = jnp.mean(ot * ot, axis=0, keepdims=True)
    ot = ot * lax.rsqrt(ms + NORM_EPS) * sub_ref[...] * (1.0 - LAMBDA_INIT)
    o_ref[...] = ot.T.astype(BF16)


def _flash_attention(lam, q, k, vt, subln_col):
    b, t, d = q.shape
    nh, nblk, vrows, tile = vt.shape[1:]
    return pl.pallas_call(
        functools.partial(_flash_kernel, tile=tile),
        grid=(b, nh, nblk),
        in_specs=[pl.BlockSpec((1, 1), lambda bi, h, i: (0, 0)),
                  pl.BlockSpec((None, tile, HEAD_WIDTH), lambda bi, h, i: (bi, i, h)),
                  pl.BlockSpec((None, t, HEAD_WIDTH), lambda bi, h, i: (bi, 0, h)),
                  pl.BlockSpec((None, None, nblk, vrows, tile), lambda bi, h, i: (bi, h, 0, 0, 0)),
                  pl.BlockSpec((HEAD_WIDTH, 1), lambda bi, h, i: (0, 0))],
        out_specs=pl.BlockSpec((None, tile, HEAD_WIDTH), lambda bi, h, i: (bi, i, h)),
        out_shape=jax.ShapeDtypeStruct((b, t, d), BF16),
        scratch_shapes=[pltpu.VMEM((2, vrows, tile), F32), pltpu.VMEM((2, tile, tile), F32),
                        pltpu.VMEM((2, tile, tile), F32), pltpu.VMEM((2, 1, tile), F32)],
        compiler_params=_params(("arbitrary", "arbitrary", "arbitrary")),
        name="flash_diff_attention",
    )(lam, q, k, vt, subln_col)


def _lane_groups(x):
    return [x[:, g * HEAD_WIDTH:(g + 1) * HEAD_WIDTH] for g in range(x.shape[1] // HEAD_WIDTH)]


def _paged_kernel(pt_ref, lam_ref, qm_ref, knew_ref, vnew_ref, sub_ref, bias_ref, nbias_ref, *refs, npages, nq, nh):
    kp_refs, vp_refs = refs[:npages], refs[npages:2 * npages]
    o_ref, s_ref = refs[2 * npages:]
    qm = qm_ref[...]
    flat = lambda ref: ref[...].reshape(-1, HEAD_WIDTH).astype(BF16)

    kn = flat(knew_ref)
    s_new = _dot_nt(qm, kn) + nbias_ref[...]
    top = None
    for p in range(npages):
        s = _dot_nt(qm, flat(kp_refs[p])) + bias_ref[...]
        s_ref[p] = s
        page_top = functools.reduce(jnp.maximum, _lane_groups(s))
        top = page_top if top is None else jnp.maximum(top, page_top)
    m = jnp.maximum(jnp.max(top, axis=1, keepdims=True), jnp.max(s_new, axis=1, keepdims=True))
    m_lanes = jnp.broadcast_to(m, (m.shape[0], HEAD_WIDTH))

    p_new = jnp.exp2(s_new - m)
    acc = _dot(p_new.astype(BF16), flat(vnew_ref))
    lsum = None
    for p in range(npages):
        groups = [jnp.exp2(g - m_lanes) for g in _lane_groups(s_ref[p])]
        part = functools.reduce(jnp.add, groups)
        lsum = part if lsum is None else lsum + part
        acc = acc + _dot(jnp.concatenate(groups, axis=1).astype(BF16), flat(vp_refs[p]))
    l = jnp.sum(lsum, axis=1, keepdims=True) + jnp.sum(p_new, axis=1, keepdims=True)
    n = (acc / l).reshape(nh, 2 * nq, HEAD_WIDTH)
    o = n[:, 0:nq, :] - lam_ref[...] * n[:, nq:2 * nq, :]
    o_ref[...] = _finish_heads(o, sub_ref[...]).astype(BF16)


def _paged_attention(page_table, lam, qm, k_new, v_new, subln, cache_k, cache_v):
    bs, npages = page_table.shape
    _, nq, nh, _ = k_new.shape
    page_keys = cache_k.shape[1]
    nrow = qm.shape[1]
    page_bytes = page_keys * nh * HEAD_WIDTH * 4
    assert 2 * 2 * npages * page_bytes <= PAGED_VMEM_BUDGET, "all pages of a sequence must fit one grid step"

    row_head = jnp.arange(nrow)[:, None] // (2 * nq)
    row_query = jnp.arange(nrow)[:, None] % nq
    col = jnp.arange(page_keys * nh)[None, :]
    bias = jnp.where(col % nh == row_head, 0.0, NEG_INF).astype(F32)
    ncol = jnp.arange(nq * nh)[None, :]
    nbias = jnp.where((ncol % nh == row_head) & (ncol // nh <= row_query), 0.0, NEG_INF).astype(F32)

    page_block = (None,) + cache_k.shape[1:]
    page_spec = lambda p: pl.BlockSpec(page_block, lambda b, pt: (pt[b, p], 0, 0, 0))
    seq4 = lambda shape: pl.BlockSpec((None,) + shape[1:], lambda b, pt: (b, 0, 0, 0))
    const2 = lambda shape: pl.BlockSpec(shape, lambda b, pt: (0, 0))
    grid_spec = pltpu.PrefetchScalarGridSpec(
        num_scalar_prefetch=1,
        grid=(bs,),
        in_specs=[const2((1, 1)),
                  pl.BlockSpec((None,) + qm.shape[1:], lambda b, pt: (b, 0, 0)),
                  seq4(k_new.shape), seq4(v_new.shape), const2((1, HEAD_WIDTH)), const2(bias.shape),
                  const2(nbias.shape)]
                 + [page_spec(p) for p in range(npages)] * 2,
        out_specs=pl.BlockSpec((None, nh, nq, HEAD_WIDTH), lambda b, pt: (b, 0, 0, 0)),
        scratch_shapes=[pltpu.VMEM((npages, nrow, page_keys * nh), F32)],
    )
    return pl.pallas_call(
        functools.partial(_paged_kernel, npages=npages, nq=nq, nh=nh),
        grid_spec=grid_spec,
        out_shape=jax.ShapeDtypeStruct((bs, nh, nq, HEAD_WIDTH), BF16),
        compiler_params=_params(("arbitrary",)),
        name="paged_diff_attention",
    )(page_table, lam, qm, k_new, v_new, subln, bias, nbias, *([cache_k] * npages), *([cache_v] * npages))


def _lam_kernel(q1_ref, k1_ref, q2_ref, k2_ref, o_ref):
    dot1 = jnp.sum(q1_ref[...] * k1_ref[...], axis=-1, keepdims=True)
    dot2 = jnp.sum(q2_ref[...] * k2_ref[...], axis=-1, keepdims=True)
    o_ref[...] = jnp.exp(dot1) - jnp.exp(dot2) + LAMBDA_INIT


def _rope_tables(pos):
    half = HEAD_DIM // 2
    inv = ROPE_THETA ** (-jnp.arange(half, dtype=F32) / half)
    ang = pos.astype(F32)[:, None] * inv[None, :]
    cos, sin = jnp.cos(ang), jnp.sin(ang)
    return jnp.tile(cos, (1, 4)), jnp.tile(jnp.concatenate([-sin, sin], axis=1), (1, 2))


def _mixer(i, x, stride, pos0, st, attend, p):
    groups, r, d = x.shape
    g = p['mix_norm'][i]
    if i == 0:
        x, pool = _pool_mixer(x, st['pool'], g, p['pool_w'], p['pool_scale'], stride, pos0)
        return x, dict(pool=pool), None
    if i == 1:
        x, rgc, rgh = _rglru_mixer(
            x, st['rgc'], st['rgh'], g, p['rg_w_gate'], p['rg_w_in'], p['rg_conv_w'], p['rg_conv_b'], p['rg_wa'],
            p['rg_ba'], p['rg_wx'], p['rg_bx'], p['rg_lambda'], p['rg_w_out'], stride)
        return x, dict(rgc=rgc, rgh=rgh), None
    if i == 2:
        x, cv = _conv_mixer(x, st['cv'], g, p['cv_w_pw1'], p['cv_b_pw1'], p['cv_dw_w'], p['cv_dw_b'],
                            p['cv_ln_g'], p['cv_ln_b'], p['cv_w_pw2'], p['cv_b_pw2'], stride)
        return x, dict(cv=cv), None
    pos = pos0 + jnp.arange(r, dtype=jnp.int32) // stride
    cos, sin = _rope_tables(pos)
    q, k, v, *flash_kv = _qkv(x.reshape(groups * r, d), g, p['at_w_qkv'], p['at_q_norm'], p['at_k_norm'], cos, sin,
                              p['seg'], for_flash=stride == 1)
    return x, dict(k=k, v=v), attend(q, k, v, *flash_kv)


def _trunk(groups, p):
    xs = [grp['x'] for grp in groups]
    outs = [dict() for _ in groups]
    nlayers = p['mix_norm'].shape[0]

    def ffn_weights(which, i):
        return (i,) + tuple(p[f'{which}_{name}'] for name in ('norm', 'w_gate', 'w_up', 'w_down'))

    def run_ffn(xs, ffns, proj=None):
        ys = _ffn([x.reshape(-1, x.shape[-1]) for x in xs], ffns, proj=proj)
        return [y.reshape(x.shape) for x, y in zip(xs, ys)]

    xs = run_ffn(xs, [ffn_weights('ffn1', 0)])
    for i in range(nlayers):
        heads = []
        for n, grp in enumerate(groups):
            xs[n], new_state, pending = _mixer(i, xs[n], grp['stride'], grp['pos0'], grp['state'], grp['attend'], p)
            outs[n].update(new_state)
            heads.append(pending)
        proj = (heads, p['at_w_o']) if heads[0] is not None else None
        ffns = [ffn_weights('ffn2', i)] + ([ffn_weights('ffn1', i + 1)] if i + 1 < nlayers else [])
        xs = run_ffn(xs, ffns, proj=proj)
    return xs, outs


def kernel(x_prompt, x_sample, state_pool, state_rglru_conv, state_rglru_h, state_conv, cache_k, cache_v, page_table, ffn1_norm, ffn1_w_gate, ffn1_w_up, ffn1_w_down, mix_norm, ffn2_norm, ffn2_w_gate, ffn2_w_up, ffn2_w_down, pool_w, pool_scale, rg_w_gate, rg_w_in, rg_conv_w, rg_conv_b, rg_wa, rg_ba, rg_wx, rg_bx, rg_lambda, rg_w_out, cv_w_pw1, cv_b_pw1, cv_dw_w, cv_dw_b, cv_ln_g, cv_ln_b, cv_w_pw2, cv_b_pw2, at_w_qkv, at_q_norm, at_k_norm, at_lam_q1, at_lam_k1, at_lam_q2, at_lam_k2, at_subln, at_w_o):
    bp, tp, d = x_prompt.shape
    bs, ts, _ = x_sample.shape
    nh = d // HEAD_WIDTH
    past = page_table.shape[1] * cache_k.shape[1]
    bf = lambda w: w.astype(BF16)
    row = lambda v: v.reshape(1, -1).astype(F32)
    rows = lambda v: v.reshape(v.shape[0], 1, -1).astype(F32)

    seg_id = jnp.arange(HEAD_WIDTH) // HEAD_DIM
    p = dict(
        ffn1_norm=rows(ffn1_norm), ffn1_w_gate=bf(ffn1_w_gate), ffn1_w_up=bf(ffn1_w_up), ffn1_w_down=bf(ffn1_w_down),
        mix_norm=rows(mix_norm),
        ffn2_norm=rows(ffn2_norm), ffn2_w_gate=bf(ffn2_w_gate), ffn2_w_up=bf(ffn2_w_up), ffn2_w_down=bf(ffn2_w_down),
        pool_w=bf(pool_w), pool_scale=row(pool_scale),
        rg_w_gate=bf(rg_w_gate), rg_w_in=bf(rg_w_in), rg_conv_w=rg_conv_w, rg_conv_b=row(rg_conv_b),
        rg_wa=bf(rg_wa), rg_ba=row(rg_ba), rg_wx=bf(rg_wx), rg_bx=row(rg_bx), rg_lambda=row(rg_lambda),
        rg_w_out=bf(rg_w_out),
        cv_w_pw1=bf(cv_w_pw1), cv_b_pw1=row(cv_b_pw1), cv_dw_w=cv_dw_w, cv_dw_b=row(cv_dw_b), cv_ln_g=row(cv_ln_g),
        cv_ln_b=row(cv_ln_b), cv_w_pw2=bf(cv_w_pw2), cv_b_pw2=row(cv_b_pw2),
        at_w_qkv=bf(at_w_qkv), at_q_norm=row(jnp.tile(at_q_norm, 2)), at_k_norm=row(jnp.tile(at_k_norm, 2)),
        at_w_o=bf(at_w_o),
        seg=((seg_id[:, None] == seg_id[None, :]).astype(F32) / HEAD_DIM).astype(BF16),
    )
    subln = row(at_subln)
    lam = pl.pallas_call(
        _lam_kernel,
        in_specs=[_const_spec((1, HEAD_DIM))] * 4,
        out_specs=_const_spec((1, 1)),
        out_shape=jax.ShapeDtypeStruct((1, 1), F32),
        name="diff_lambda",
    )(row(at_lam_q1), row(at_lam_k1), row(at_lam_q2), row(at_lam_k2))

    def attend_prompt(q, k, v, kb, vt):
        shape = (bp, tp, d)
        return _flash_attention(lam, q.reshape(shape), kb.reshape(shape), vt, subln.reshape(-1, 1)).reshape(-1, d)

    zeros = lambda n: jnp.zeros((bp, n, d), F32)
    prompt = dict(x=x_prompt, stride=1, pos0=0, attend=attend_prompt,
                  state=dict(pool=zeros(POOL_BUF), rgc=zeros(RG_CONV - 1), rgh=zeros(1), cv=zeros(CV_WIDTH - 1)))

    def to_tm(a):
        return jnp.swapaxes(a, 0, 1).reshape(1, a.shape[1] * bs, d)

    def from_tm(a):
        return jnp.swapaxes(a.reshape(-1, bs, d), 0, 1)

    def attend_sample(q, k, v):
        q5 = from_tm(q).reshape(bs, ts, nh, 2, HEAD_DIM)
        eye = jnp.eye(2, dtype=q.dtype)
        qm = jnp.einsum('bqhcd,ce->bhcqed', q5, eye).reshape(bs, nh * 2 * ts, HEAD_WIDTH)
        k_new = from_tm(k).reshape(bs, ts, nh, HEAD_WIDTH)
        v_new = from_tm(v).reshape(bs, ts, nh, HEAD_WIDTH)
        o = _paged_attention(page_table, lam, qm, k_new, v_new, subln, cache_k, cache_v)
        return jnp.transpose(o, (2, 0, 1, 3)).reshape(ts * bs, d)

    sample = dict(x=to_tm(x_sample), stride=bs, pos0=past, attend=attend_sample,
                  state=dict(pool=to_tm(state_pool), rgc=to_tm(state_rglru_conv),
                             rgh=state_rglru_h.reshape(1, bs, d), cv=to_tm(state_conv)))
    (y_p, y_s), (o_p, o_s) = _trunk([prompt, sample], p)

    kv4 = lambda a, b, t: a.reshape(b, t, nh, HEAD_WIDTH)
    return (y_p, from_tm(y_s),
            o_p['pool'], from_tm(o_s['pool']),
            o_p['rgc'], from_tm(o_s['rgc']),
            o_p['rgh'].reshape(bp, d), o_s['rgh'].reshape(bs, d),
            o_p['cv'], from_tm(o_s['cv']),
            kv4(o_p['k'], bp, tp), kv4(o_p['v'], bp, tp),
            kv4(from_tm(o_s['k']), bs, ts), kv4(from_tm(o_s['v']), bs, ts))
```

```python
import functools
import math

import jax
import jax.numpy as jnp
from jax import lax
from jax.experimental import pallas as pl
from jax.experimental.pallas import tpu as pltpu

F32 = jnp.float32
BF16 = jnp.bfloat16

NORM_EPS = 1e-6
POOL_WINDOWS = (2, 4, 8, 16)
POOL_BUF = max(POOL_WINDOWS) - 1
RG_CONV = 4
RG_C = 8.0
CV_WIDTH = 31
HEAD_DIM = 64
HEAD_WIDTH = 2 * HEAD_DIM
ROPE_THETA = 10000.0
ATT_LAYER = 3
LAMBDA_INIT = 0.8 - 0.6 * math.exp(-0.3 * ATT_LAYER)
NEG_INF = -1e30

Q_SCALE = HEAD_DIM ** -0.5 * math.log2(math.e)

SUBLANES = 8
VMEM_LIMIT = 56 * 1024 * 1024
PAGED_VMEM_BUDGET = 36 * 1024 * 1024
SEQ_TILE = 512
FFN_CHUNK = 256
FLASH_UNROLL = 4
VT_ROWS = HEAD_WIDTH + 16


def _params(sem):
    return pltpu.CompilerParams(dimension_semantics=sem, vmem_limit_bytes=VMEM_LIMIT)


def _rms(x, g):
    ms = jnp.mean(x * x, axis=-1, keepdims=True)
    return x * lax.rsqrt(ms + NORM_EPS) * g


def _dot(a, b):
    return jnp.dot(a, b, preferred_element_type=F32)


def _dot_nt(a, b):
    return lax.dot_general(a, b, (((1,), (1,)), ((), ())), preferred_element_type=F32)


def _const_spec(shape):
    n = len(shape)
    return pl.BlockSpec(shape, lambda *_: (0,) * n)


def _ffn_kernel(*refs, nstreams, tiles, with_proj, nffn):
    per = 2 if with_proj else 1
    ins = refs[:nstreams * per]
    rest = refs[nstreams * per:]
    if with_proj:
        wo_ref, rest = rest[0], rest[1:]
    ffns = [rest[4 * n:4 * n + 4] for n in range(nffn)]
    outs = rest[4 * nffn:4 * nffn + nstreams]
    h_ref = rest[4 * nffn + nstreams]
    i = pl.program_id(0)
    start = 0
    for s in range(nstreams):
        @pl.when((i >= start) & (i < start + tiles[s]))
        def _(s=s):
            x = ins[s * per][...]
            if with_proj:
                x = x + _dot(ins[s * per + 1][...], wo_ref[...])
            rows = x.shape[0]
            for g_ref, wg_ref, wu_ref, wd_ref in ffns:
                u = _rms(x, g_ref[...]).astype(BF16)
                for c in range(wg_ref.shape[1] // FFN_CHUNK):
                    sl = slice(c * FFN_CHUNK, (c + 1) * FFN_CHUNK)
                    gate = _dot(u, wg_ref[:, sl])
                    up = _dot(u, wu_ref[:, sl])
                    h_ref[0:rows, sl] = (gate * jax.nn.sigmoid(gate) * up).astype(BF16)
                x = x + 0.5 * _dot(h_ref[0:rows, :], wd_ref[...])
            outs[s][...] = x
        start += tiles[s]


def _ffn(xs, ffns, proj=None):
    d = xs[0].shape[1]
    d_ff = ffns[0][2].shape[2]
    tms = [min(SEQ_TILE, x.shape[0]) for x in xs]
    tiles = [x.shape[0] // tm for x, tm in zip(xs, tms)]
    offsets = [sum(tiles[:s]) for s in range(len(xs))]
    single = pl.Buffered(1)

    def stream_spec(s, width):
        return pl.BlockSpec((tms[s], width),
                            lambda i, s=s: (jnp.clip(i - offsets[s], 0, tiles[s] - 1), 0))

    args, specs = [], []
    for s, x in enumerate(xs):
        args.append(x)
        specs.append(stream_spec(s, d))
        if proj is not None:
            args.append(proj[0][s])
            specs.append(stream_spec(s, proj[0][s].shape[1]))
    if proj is not None:
        args.append(proj[1])
        specs.append(pl.BlockSpec(proj[1].shape, lambda i: (0, 0), pipeline_mode=single))
    for layer, g, wg, wu, wd in ffns:
        pick = lambda i, layer=layer: (layer, 0, 0)
        args += [g, wg, wu, wd]
        specs += [pl.BlockSpec((None, 1, d), pick),
                  pl.BlockSpec((None, d, d_ff), pick, pipeline_mode=single),
                  pl.BlockSpec((None, d, d_ff), pick, pipeline_mode=single),
                  pl.BlockSpec((None, d_ff, d), pick, pipeline_mode=single)]
    return pl.pallas_call(
        functools.partial(_ffn_kernel, nstreams=len(xs), tiles=tuple(tiles), with_proj=proj is not None,
                          nffn=len(ffns)),
        grid=(sum(tiles),),
        in_specs=specs,
        out_specs=[stream_spec(s, d) for s in range(len(xs))],
        out_shape=[jax.ShapeDtypeStruct(x.shape, F32) for x in xs],
        scratch_shapes=[pltpu.VMEM((max(tms), d_ff), BF16)],
        compiler_params=_params(("arbitrary",)),
        name="ffn_proj" if proj is not None else "ffn",
    )(*args)


def _load_history(ext_ref, prev_ref, halo_rows):
    pb = prev_ref.shape[0]
    if halo_rows > pb:
        ext_ref[0:halo_rows - pb, :] = jnp.zeros((halo_rows - pb, ext_ref.shape[1]), F32)
    ext_ref[halo_rows - pb:halo_rows, :] = prev_ref[...]


def _pool_kernel(x_ref, prev_ref, g_ref, w_ref, sc_ref, o_ref, st_ref, ext_ref, *, stride, tt, halo, pos0, nsteps):
    t = pl.program_id(1)
    rows, hr = tt * stride, halo * stride
    pb = prev_ref.shape[0]

    @pl.when(t == 0)
    def _():
        _load_history(ext_ref, prev_ref, hr)

    x = x_ref[...]
    u = _rms(x, g_ref[...])
    ext_ref[hr:hr + rows, :] = u
    step = lax.broadcasted_iota(jnp.int32, (rows, 1), 0) >> int(math.log2(stride))
    pos = pos0 + t * tt + step
    gw = x.shape[1] // len(POOL_WINDOWS)
    for gi, win in enumerate(POOL_WINDOWS):
        cs = slice(gi * gw, (gi + 1) * gw)
        s = u[:, cs]
        for k in range(1, win):
            s = s + ext_ref[hr - k * stride:hr - k * stride + rows, cs]
        cnt = jnp.minimum(pos + 1, win).astype(F32)
        diff = (s / cnt - u[:, cs]).astype(BF16)
        o_ref[:, cs] = x[:, cs] + _dot(diff, w_ref[gi]) * sc_ref[:, cs]

    @pl.when(t == nsteps - 1)
    def _():
        st_ref[...] = ext_ref[hr + rows - pb:hr + rows, :]

    if nsteps > 1:
        ext_ref[0:hr, :] = ext_ref[rows:rows + hr, :]


def _seq_specs(rows, d):
    return pl.BlockSpec((None, rows, d), lambda g, t: (g, t, 0))


def _state_spec(shape):
    return pl.BlockSpec((None,) + tuple(shape[1:]), lambda g, t: (g,) + (0,) * (len(shape) - 1))


def _tiling(x, stride):
    groups, r, d = x.shape
    tt = min(SEQ_TILE, r) if stride == 1 else r // stride
    rows = tt * stride
    return groups, r, d, tt, rows, r // rows


def _pool_mixer(x, prev, g, w, scale, stride, pos0):
    groups, r, d, tt, rows, nsteps = _tiling(x, stride)
    halo = 16 if stride == 1 else POOL_BUF
    return pl.pallas_call(
        functools.partial(_pool_kernel, stride=stride, tt=tt, halo=halo, pos0=pos0, nsteps=nsteps),
        grid=(groups, nsteps),
        in_specs=[_seq_specs(rows, d), _state_spec(prev.shape), _const_spec(g.shape), _const_spec(w.shape),
                  _const_spec(scale.shape)],
        out_specs=[_seq_specs(rows, d), _state_spec(prev.shape)],
        out_shape=[jax.ShapeDtypeStruct(x.shape, F32), jax.ShapeDtypeStruct(prev.shape, F32)],
        scratch_shapes=[pltpu.VMEM((halo * stride + rows, d), F32)],
        compiler_params=_params(("arbitrary", "arbitrary")),
        name="pool_mixer",
    )(x, prev, g, w, scale)


def _rglru_kernel(x_ref, cprev_ref, hprev_ref, g_ref, wgate_ref, win_ref, cw_ref, cb_ref, wa_ref, ba_ref, wx_ref,
                  bx_ref, lam_ref, wout_ref, o_ref, cst_ref, hst_ref, ext_ref, h_ref, a_ref, b_ref,
                  *, stride, tt, halo, nsteps):
    t = pl.program_id(1)
    rows, hr = tt * stride, halo * stride
    pb = cprev_ref.shape[0]

    @pl.when(t == 0)
    def _():
        _load_history(ext_ref, cprev_ref, hr)
        h_ref[...] = hprev_ref[...]

    x = x_ref[...]
    u = _rms(x, g_ref[...]).astype(BF16)
    gate = jax.nn.gelu(_dot(u, wgate_ref[...]))
    zin = _dot(u, win_ref[...])
    ext_ref[hr:hr + rows, :] = zin
    z = cb_ref[...] + cw_ref[RG_CONV - 1:RG_CONV, :] * zin
    for k in range(RG_CONV - 1):
        back = (RG_CONV - 1 - k) * stride
        z = z + cw_ref[k:k + 1, :] * ext_ref[hr - back:hr - back + rows, :]
    zb = z.astype(BF16)
    lam = lam_ref[...]
    softplus_neg_lam = jnp.maximum(-lam, 0.0) + jnp.log1p(jnp.exp(-jnp.abs(lam)))
    nblk = wa_ref.shape[0]
    bw = x.shape[1] // nblk
    for n in range(nblk):
        cs = slice(n * bw, (n + 1) * bw)
        r = jax.nn.sigmoid(_dot(zb[:, cs], wa_ref[n]) + ba_ref[:, cs])
        i = jax.nn.sigmoid(_dot(zb[:, cs], wx_ref[n]) + bx_ref[:, cs])
        log_a = -RG_C * r * softplus_neg_lam[:, cs]
        a = jnp.exp(log_a)
        a_ref[:, cs] = a
        b_ref[:, cs] = jnp.sqrt(1.0 - a * a) * (i * z[:, cs])

    if stride == 1:
        def chunk(c, h):
            r0 = pl.multiple_of(c * SUBLANES, SUBLANES)
            a = a_ref[pl.ds(r0, SUBLANES), :]
            b = b_ref[pl.ds(r0, SUBLANES), :]
            row = lax.broadcasted_iota(jnp.int32, a.shape, 0)
            for s in (1, 2, 4):
                keep = row >= s
                b = jnp.where(keep, a * pltpu.roll(b, s, axis=0) + b, b)
                a = jnp.where(keep, a * pltpu.roll(a, s, axis=0), a)
            hs = a * h + b
            b_ref[pl.ds(r0, SUBLANES), :] = hs
            return hs[SUBLANES - 1:SUBLANES, :]

        h_ref[...] = lax.fori_loop(0, rows // SUBLANES, chunk, h_ref[...], unroll=4)
    else:
        h = h_ref[...]
        for s in range(tt):
            sl = slice(s * stride, (s + 1) * stride)
            h = a_ref[sl, :] * h + b_ref[sl, :]
            b_ref[sl, :] = h
        h_ref[...] = h

    y = _dot((gate * b_ref[...]).astype(BF16), wout_ref[...])
    o_ref[...] = x + y

    @pl.when(t == nsteps - 1)
    def _():
        cst_ref[...] = ext_ref[hr + rows - pb:hr + rows, :]
        hst_ref[...] = h_ref[...]

    if nsteps > 1:
        ext_ref[0:hr, :] = ext_ref[rows:rows + hr, :]


def _rglru_mixer(x, cprev, hprev, g, wgate, win, cw, cb, wa, ba, wx, bx, lam, wout, stride):
    groups, r, d, tt, rows, nsteps = _tiling(x, stride)
    halo = SUBLANES if stride == 1 else RG_CONV - 1
    consts = [g, wgate, win, cw, cb, wa, ba, wx, bx, lam, wout]
    return pl.pallas_call(
        functools.partial(_rglru_kernel, stride=stride, tt=tt, halo=halo, nsteps=nsteps),
        grid=(groups, nsteps),
        in_specs=[_seq_specs(rows, d), _state_spec(cprev.shape), _state_spec(hprev.shape)]
                 + [_const_spec(c.shape) for c in consts],
        out_specs=[_seq_specs(rows, d), _state_spec(cprev.shape), _state_spec(hprev.shape)],
        out_shape=[jax.ShapeDtypeStruct(x.shape, F32), jax.ShapeDtypeStruct(cprev.shape, F32),
                   jax.ShapeDtypeStruct(hprev.shape, F32)],
        scratch_shapes=[pltpu.VMEM((halo * stride + rows, d), F32), pltpu.VMEM((stride, d), F32),
                        pltpu.VMEM((rows, d), F32), pltpu.VMEM((rows, d), F32)],
        compiler_params=_params(("arbitrary", "arbitrary")),
        name="rglru_mixer",
    )(x, cprev, hprev, *consts)


def _convmod_kernel(x_ref, prev_ref, g_ref, w1_ref, b1_ref, dw_ref, dwb_ref, lng_ref, lnb_ref, w2_ref, b2_ref,
                    o_ref, st_ref, ext_ref, *, stride, tt, halo, nsteps):
    t = pl.program_id(1)
    rows, hr = tt * stride, halo * stride
    pb = prev_ref.shape[0]
    d = x_ref.shape[1]

    @pl.when(t == 0)
    def _():
        _load_history(ext_ref, prev_ref, hr)

    x = x_ref[...]
    u = _rms(x, g_ref[...]).astype(BF16)
    hc = _dot(u, w1_ref[...]) + b1_ref[...]
    glu = hc[:, :d] * jax.nn.sigmoid(hc[:, d:])
    ext_ref[hr:hr + rows, :] = glu
    c = dwb_ref[...]
    if stride == 1:
        for b in range(SUBLANES):
            phase = None
            for a in range((CV_WIDTH - 1 - b) // SUBLANES + 1):
                k = CV_WIDTH - 1 - (SUBLANES * a + b)
                start = hr - SUBLANES * (a + 1)
                term = dw_ref[k:k + 1, :] * ext_ref[start:start + rows + SUBLANES, :]
                phase = term if phase is None else phase + term
            c = c + phase[SUBLANES - b:SUBLANES - b + rows, :]
    else:
        for k in range(CV_WIDTH):
            back = (CV_WIDTH - 1 - k) * stride
            c = c + dw_ref[k:k + 1, :] * ext_ref[hr - back:hr - back + rows, :]
    cc = c - jnp.mean(c, axis=-1, keepdims=True)
    y = cc * lax.rsqrt(jnp.mean(cc * cc, axis=-1, keepdims=True) + NORM_EPS)
    y = y * lng_ref[...] + lnb_ref[...]
    y = (y * jax.nn.sigmoid(y)).astype(BF16)
    o_ref[...] = x + _dot(y, w2_ref[...]) + b2_ref[...]

    @pl.when(t == nsteps - 1)
    def _():
        st_ref[...] = ext_ref[hr + rows - pb:hr + rows, :]

    if nsteps > 1:
        ext_ref[0:hr, :] = ext_ref[rows:rows + hr, :]


def _conv_mixer(x, prev, g, w1, b1, dw, dwb, lng, lnb, w2, b2, stride):
    groups, r, d, tt, rows, nsteps = _tiling(x, stride)
    halo = 32 if stride == 1 else CV_WIDTH - 1
    consts = [g, w1, b1, dw, dwb, lng, lnb, w2, b2]
    return pl.pallas_call(
        functools.partial(_convmod_kernel, stride=stride, tt=tt, halo=halo, nsteps=nsteps),
        grid=(groups, nsteps),
        in_specs=[_seq_specs(rows, d), _state_spec(prev.shape)] + [_const_spec(c.shape) for c in consts],
        out_specs=[_seq_specs(rows, d), _state_spec(prev.shape)],
        out_shape=[jax.ShapeDtypeStruct(x.shape, F32), jax.ShapeDtypeStruct(prev.shape, F32)],
        scratch_shapes=[pltpu.VMEM((halo * stride + rows, d), F32)],
        compiler_params=_params(("arbitrary", "arbitrary")),
        name="conv_mixer",
    )(x, prev, g, w1, b1, dw, dwb, lng, lnb, w2, b2)


def _qkv_kernel(x_ref, g_ref, w_ref, qg_ref, kg_ref, cos_ref, sin_ref, seg_ref, q_ref, k_ref, v_ref, *flash_refs):
    d = x_ref.shape[1]
    u = _rms(x_ref[...], g_ref[...]).astype(BF16)
    qkv = _dot(u, w_ref[...])
    cos, sin = cos_ref[...], sin_ref[...]
    seg = seg_ref[...]
    lane = lax.broadcasted_iota(jnp.int32, (1, HEAD_WIDTH), 1)
    first_half = (lane & (HEAD_DIM - 1)) < HEAD_DIM // 2

    def norm_rope(blk, gain):
        sq = blk * blk
        hi = sq.astype(BF16)
        lo = (sq - hi.astype(F32)).astype(BF16)
        ms = _dot(hi, seg) + _dot(lo, seg)
        y = blk * lax.rsqrt(ms + NORM_EPS) * gain
        partner = jnp.where(first_half, pltpu.roll(y, HEAD_WIDTH - HEAD_DIM // 2, axis=1),
                            pltpu.roll(y, HEAD_DIM // 2, axis=1))
        return y * cos + partner * sin

    for h in range(d // HEAD_WIDTH):
        cs = slice(h * HEAD_WIDTH, (h + 1) * HEAD_WIDTH)
        q = norm_rope(qkv[:, cs], qg_ref[...])
        q_ref[:, cs] = (q * Q_SCALE).astype(BF16)
        k = norm_rope(qkv[:, d + h * HEAD_WIDTH:d + (h + 1) * HEAD_WIDTH], kg_ref[...])
        k_ref[:, cs] = k
        v = qkv[:, 2 * d + h * HEAD_WIDTH:2 * d + (h + 1) * HEAD_WIDTH]
        v_ref[:, cs] = v
        if flash_refs:
            kb_ref, vt_ref = flash_refs
            kb_ref[:, cs] = k.astype(BF16)
            vt_ref[h, 0:HEAD_WIDTH, :] = v.T.astype(BF16)
            pad = lax.broadcasted_iota(jnp.int32, (VT_ROWS - HEAD_WIDTH, v.shape[0]), 0)
            vt_ref[h, HEAD_WIDTH:VT_ROWS, :] = (pad == 0).astype(BF16)


def _qkv(x, g, w, qg, kg, cos, sin, seg, for_flash):
    n, d = x.shape
    tm = min(SEQ_TILE, n)
    nblk = cos.shape[0] // tm
    nh = d // HEAD_WIDTH
    row = pl.BlockSpec((tm, d), lambda i: (i, 0))
    tab = pl.BlockSpec((tm, HEAD_WIDTH), lambda i: (i % nblk, 0))
    out_specs = [row] * 3
    out_shape = [jax.ShapeDtypeStruct((n, d), dt) for dt in (BF16, F32, F32)]
    if for_flash:
        out_specs += [row, pl.BlockSpec((None, nh, None, VT_ROWS, tm), lambda i: (i // nblk, 0, i % nblk, 0, 0))]
        out_shape += [jax.ShapeDtypeStruct((n, d), BF16),
                      jax.ShapeDtypeStruct((n // (nblk * tm), nh, nblk, VT_ROWS, tm), BF16)]
    return pl.pallas_call(
        _qkv_kernel,
        grid=(n // tm,),
        in_specs=[row, _const_spec(g.shape), _const_spec(w.shape), _const_spec(qg.shape), _const_spec(kg.shape),
                  tab, tab, _const_spec(seg.shape)],
        out_specs=out_specs,
        out_shape=out_shape,
        compiler_params=_params(("arbitrary",)),
        name="qkv_rope",
    )(x, g, w, qg, kg, cos, sin, seg)


def _finish_heads(o, subln):
    ms = jnp.mean(o * o, axis=-1, keepdims=True)
    return o * lax.rsqrt(ms + NORM_EPS) * subln * (1.0 - LAMBDA_INIT)


def _flash_body(i, lam_ref, q_ref, k_ref, vt_ref, sub_ref, o_ref, acc_ref, sa_ref, sb_ref, m_ref, *, tile):
    q = q_ref[...]
    lane = lax.broadcasted_iota(jnp.int32, q.shape, 1)
    qh = (jnp.where(lane < HEAD_DIM, q, jnp.zeros_like(q)), jnp.where(lane >= HEAD_DIM, q, jnp.zeros_like(q)))
    acc_ref[...] = jnp.zeros(acc_ref.shape, F32)
    m_ref[...] = jnp.full(m_ref.shape, NEG_INF, F32)

    def produce(j, s_ref, c):
        k = k_ref[pl.ds(pl.multiple_of(j * tile, tile), tile), :]
        s_ref[c] = _dot_nt(k, qh[c])

    def consume(j, s_ref, c, masked):
        st = s_ref[c]
        if masked:
            visible = (lax.broadcasted_iota(jnp.int32, (tile, tile), 0)
                       <= lax.broadcasted_iota(jnp.int32, (tile, tile), 1))
            st = jnp.where(visible, st, NEG_INF)
        m_old = m_ref[c]
        m_new = jnp.maximum(m_old, jnp.max(st, axis=0, keepdims=True))
        pt = jnp.exp2(st - m_new).astype(BF16)
        acc_ref[c] = jnp.exp2(m_old - m_new) * acc_ref[c] + _dot(vt_ref[j], pt)
        m_ref[c] = m_new

    bufs = (sa_ref, sb_ref)

    def steps(j0, n, slot):
        for u in range(n):
            for c in range(2):
                produce(j0 + u + 1, bufs[(slot + u + 1) % 2], c)
                consume(j0 + u, bufs[(slot + u) % 2], c, False)

    for c in range(2):
        produce(0, sa_ref, c)

    def group(g, carry):
        steps(g * FLASH_UNROLL, FLASH_UNROLL, 0)
        return carry

    lax.fori_loop(0, i // FLASH_UNROLL, group, 0)
    for r in range(FLASH_UNROLL):
        @pl.when(i % FLASH_UNROLL == r)
        def _(r=r):
            steps(i - r, r, 0)
            for c in range(2):
                consume(i, bufs[r % 2], c, True)

    a0, a1 = acc_ref[0], acc_ref[1]
    ot = (a0[:HEAD_WIDTH] * (1.0 / a0[HEAD_WIDTH:HEAD_WIDTH + 1])
          - a1[:HEAD_WIDTH] * (lam_ref[...] / a1[HEAD_WIDTH:HEAD_WIDTH + 1]))
    ms = jnp.mean(ot * ot, axis=0, keepdims=True)
    ot = ot * lax.rsqrt(ms + NORM_EPS) * sub_ref[...] * (1.0 - LAMBDA_INIT)
    o_ref[...] = ot.T.astype(BF16)


def _lane_groups(x):
    return [x[:, g * HEAD_WIDTH:(g + 1) * HEAD_WIDTH] for g in range(x.shape[1] // HEAD_WIDTH)]


def _paged_body(lam_ref, qm_ref, knew_ref, vnew_ref, sub_ref, bias_ref, nbias_ref, kp_refs, vp_refs, o_ref, s_ref,
                *, nq, nh):
    npages = len(kp_refs)
    qm = qm_ref[...]
    flat = lambda ref: ref[...].reshape(-1, HEAD_WIDTH).astype(BF16)

    kn = flat(knew_ref)
    s_new = _dot_nt(qm, kn) + nbias_ref[...]
    top = None
    for p in range(npages):
        s = _dot_nt(qm, flat(kp_refs[p])) + bias_ref[...]
        s_ref[p] = s
        page_top = functools.reduce(jnp.maximum, _lane_groups(s))
        top = page_top if top is None else jnp.maximum(top, page_top)
    m = jnp.maximum(jnp.max(top, axis=1, keepdims=True), jnp.max(s_new, axis=1, keepdims=True))
    m_lanes = jnp.broadcast_to(m, (m.shape[0], HEAD_WIDTH))

    p_new = jnp.exp2(s_new - m)
    acc = _dot(p_new.astype(BF16), flat(vnew_ref))
    lsum = None
    for p in range(npages):
        groups = [jnp.exp2(g - m_lanes) for g in _lane_groups(s_ref[p])]
        part = functools.reduce(jnp.add, groups)
        lsum = part if lsum is None else lsum + part
        acc = acc + _dot(jnp.concatenate(groups, axis=1).astype(BF16), flat(vp_refs[p]))
    l = jnp.sum(lsum, axis=1, keepdims=True) + jnp.sum(p_new, axis=1, keepdims=True)
    n = (acc / l).reshape(nh, 2 * nq, HEAD_WIDTH)
    o = n[:, 0:nq, :] - lam_ref[...] * n[:, nq:2 * nq, :]
    o_ref[...] = _finish_heads(o, sub_ref[...]).astype(BF16)


def _attention_kernel(pt_ref, lam_ref, q_ref, k_ref, vt_ref, subc_ref, qm_ref, knew_ref, vnew_ref, subr_ref, bias_ref,
                      nbias_ref, *refs, tile, npages, nq, nh, nseq):
    kp_refs, vp_refs = refs[:npages], refs[npages:2 * npages]
    o_ref, po_ref, acc_ref, sa_ref, sb_ref, m_ref, ps_ref = refs[2 * npages:]
    step = (pl.program_id(0) * pl.num_programs(1) + pl.program_id(1)) * pl.num_programs(2) + pl.program_id(2)

    @pl.when(step < nseq)
    def _():
        _paged_body(lam_ref, qm_ref, knew_ref, vnew_ref, subr_ref, bias_ref, nbias_ref, kp_refs, vp_refs, po_ref,
                    ps_ref, nq=nq, nh=nh)

    _flash_body(pl.program_id(2), lam_ref, q_ref, k_ref, vt_ref, subc_ref, o_ref, acc_ref, sa_ref, sb_ref, m_ref,
                tile=tile)


def _attention(page_table, lam, subln, q, k, vt, qm, k_new, v_new, cache_k, cache_v):
    b, t, d = q.shape
    nblk, vrows, tile = vt.shape[2:]
    bs, npages = page_table.shape
    _, nq, nh, _ = k_new.shape
    page_keys = cache_k.shape[1]
    nrow = qm.shape[1]
    page_bytes = page_keys * nh * HEAD_WIDTH * 4
    assert 2 * 2 * npages * page_bytes <= PAGED_VMEM_BUDGET, "all pages of a sequence must fit one grid step"
    assert bs <= b * nh * nblk, "one paged sequence per grid step"

    row_head = jnp.arange(nrow)[:, None] // (2 * nq)
    row_query = jnp.arange(nrow)[:, None] % nq
    col = jnp.arange(page_keys * nh)[None, :]
    bias = jnp.where(col % nh == row_head, 0.0, NEG_INF).astype(F32)
    ncol = jnp.arange(nq * nh)[None, :]
    nbias = jnp.where((ncol % nh == row_head) & (ncol // nh <= row_query), 0.0, NEG_INF).astype(F32)

    seq = lambda bi, h, i: jnp.minimum((bi * nh + h) * nblk + i, bs - 1)
    page_block = (None,) + cache_k.shape[1:]
    page_spec = lambda p: pl.BlockSpec(page_block, lambda bi, h, i, pt: (pt[seq(bi, h, i), p], 0, 0, 0))
    seq4 = lambda shape: pl.BlockSpec((None,) + shape[1:], lambda bi, h, i, pt: (seq(bi, h, i), 0, 0, 0))
    const2 = lambda shape: pl.BlockSpec(shape, lambda bi, h, i, pt: (0, 0))
    grid_spec = pltpu.PrefetchScalarGridSpec(
        num_scalar_prefetch=1,
        grid=(b, nh, nblk),
        in_specs=[const2((1, 1)),
                  pl.BlockSpec((None, tile, HEAD_WIDTH), lambda bi, h, i, pt: (bi, i, h)),
                  pl.BlockSpec((None, t, HEAD_WIDTH), lambda bi, h, i, pt: (bi, 0, h)),
                  pl.BlockSpec((None, None, nblk, vrows, tile), lambda bi, h, i, pt: (bi, h, 0, 0, 0)),
                  const2((HEAD_WIDTH, 1)),
                  pl.BlockSpec((None,) + qm.shape[1:], lambda bi, h, i, pt: (seq(bi, h, i), 0, 0)),
                  seq4(k_new.shape), seq4(v_new.shape), const2((1, HEAD_WIDTH)), const2(bias.shape),
                  const2(nbias.shape)]
                 + [page_spec(p) for p in range(npages)] * 2,
        out_specs=[pl.BlockSpec((None, tile, HEAD_WIDTH), lambda bi, h, i, pt: (bi, i, h)),
                   pl.BlockSpec((None, nh, nq, HEAD_WIDTH), lambda bi, h, i, pt: (seq(bi, h, i), 0, 0, 0))],
        scratch_shapes=[pltpu.VMEM((2, vrows, tile), F32), pltpu.VMEM((2, tile, tile), F32),
                        pltpu.VMEM((2, tile, tile), F32), pltpu.VMEM((2, 1, tile), F32),
                        pltpu.VMEM((npages, nrow, page_keys * nh), F32)],
    )
    return pl.pallas_call(
        functools.partial(_attention_kernel, tile=tile, npages=npages, nq=nq, nh=nh, nseq=bs),
        grid_spec=grid_spec,
        out_shape=[jax.ShapeDtypeStruct((b, t, d), BF16), jax.ShapeDtypeStruct((bs, nh, nq, HEAD_WIDTH), BF16)],
        compiler_params=_params(("arbitrary", "arbitrary", "arbitrary")),
        name="diff_attention",
    )(page_table, lam, q, k, vt, subln.reshape(-1, 1), qm, k_new, v_new, subln, bias, nbias,
      *([cache_k] * npages), *([cache_v] * npages))


def _lam_kernel(q1_ref, k1_ref, q2_ref, k2_ref, o_ref):
    dot1 = jnp.sum(q1_ref[...] * k1_ref[...], axis=-1, keepdims=True)
    dot2 = jnp.sum(q2_ref[...] * k2_ref[...], axis=-1, keepdims=True)
    o_ref[...] = jnp.exp(dot1) - jnp.exp(dot2) + LAMBDA_INIT


def _rope_tables(pos):
    half = HEAD_DIM // 2
    inv = ROPE_THETA ** (-jnp.arange(half, dtype=F32) / half)
    ang = pos.astype(F32)[:, None] * inv[None, :]
    cos, sin = jnp.cos(ang), jnp.sin(ang)
    return jnp.tile(cos, (1, 4)), jnp.tile(jnp.concatenate([-sin, sin], axis=1), (1, 2))


def _mixer(i, x, stride, pos0, st, p):
    groups, r, d = x.shape
    g = p['mix_norm'][i]
    if i == 0:
        x, pool = _pool_mixer(x, st['pool'], g, p['pool_w'], p['pool_scale'], stride, pos0)
        return x, dict(pool=pool), None
    if i == 1:
        x, rgc, rgh = _rglru_mixer(
            x, st['rgc'], st['rgh'], g, p['rg_w_gate'], p['rg_w_in'], p['rg_conv_w'], p['rg_conv_b'], p['rg_wa'],
            p['rg_ba'], p['rg_wx'], p['rg_bx'], p['rg_lambda'], p['rg_w_out'], stride)
        return x, dict(rgc=rgc, rgh=rgh), None
    if i == 2:
        x, cv = _conv_mixer(x, st['cv'], g, p['cv_w_pw1'], p['cv_b_pw1'], p['cv_dw_w'], p['cv_dw_b'],
                            p['cv_ln_g'], p['cv_ln_b'], p['cv_w_pw2'], p['cv_b_pw2'], stride)
        return x, dict(cv=cv), None
    pos = pos0 + jnp.arange(r, dtype=jnp.int32) // stride
    cos, sin = _rope_tables(pos)
    q, k, v, *flash_kv = _qkv(x.reshape(groups * r, d), g, p['at_w_qkv'], p['at_q_norm'], p['at_k_norm'], cos, sin,
                              p['seg'], for_flash=stride == 1)
    return x, dict(k=k, v=v), (q, k, v, *flash_kv)


def _trunk(groups, attend, p):
    xs = [grp['x'] for grp in groups]
    outs = [dict() for _ in groups]
    nlayers = p['mix_norm'].shape[0]

    def ffn_weights(which, i):
        return (i,) + tuple(p[f'{which}_{name}'] for name in ('norm', 'w_gate', 'w_up', 'w_down'))

    def run_ffn(xs, ffns, proj=None):
        ys = _ffn([x.reshape(-1, x.shape[-1]) for x in xs], ffns, proj=proj)
        return [y.reshape(x.shape) for x, y in zip(xs, ys)]

    xs = run_ffn(xs, [ffn_weights('ffn1', 0)])
    for i in range(nlayers):
        qkvs = []
        for n, grp in enumerate(groups):
            xs[n], new_state, qkv = _mixer(i, xs[n], grp['stride'], grp['pos0'], grp['state'], p)
            outs[n].update(new_state)
            qkvs.append(qkv)
        proj = (attend(qkvs), p['at_w_o']) if qkvs[0] is not None else None
        ffns = [ffn_weights('ffn2', i)] + ([ffn_weights('ffn1', i + 1)] if i + 1 < nlayers else [])
        xs = run_ffn(xs, ffns, proj=proj)
    return xs, outs


def kernel(x_prompt, x_sample, state_pool, state_rglru_conv, state_rglru_h, state_conv, cache_k, cache_v, page_table, ffn1_norm, ffn1_w_gate, ffn1_w_up, ffn1_w_down, mix_norm, ffn2_norm, ffn2_w_gate, ffn2_w_up, ffn2_w_down, pool_w, pool_scale, rg_w_gate, rg_w_in, rg_conv_w, rg_conv_b, rg_wa, rg_ba, rg_wx, rg_bx, rg_lambda, rg_w_out, cv_w_pw1, cv_b_pw1, cv_dw_w, cv_dw_b, cv_ln_g, cv_ln_b, cv_w_pw2, cv_b_pw2, at_w_qkv, at_q_norm, at_k_norm, at_lam_q1, at_lam_k1, at_lam_q2, at_lam_k2, at_subln, at_w_o):
    bp, tp, d = x_prompt.shape
    bs, ts, _ = x_sample.shape
    nh = d // HEAD_WIDTH
    past = page_table.shape[1] * cache_k.shape[1]
    bf = lambda w: w.astype(BF16)
    row = lambda v: v.reshape(1, -1).astype(F32)
    rows = lambda v: v.reshape(v.shape[0], 1, -1).astype(F32)

    seg_id = jnp.arange(HEAD_WIDTH) // HEAD_DIM
    p = dict(
        ffn1_norm=rows(ffn1_norm), ffn1_w_gate=bf(ffn1_w_gate), ffn1_w_up=bf(ffn1_w_up), ffn1_w_down=bf(ffn1_w_down),
        mix_norm=rows(mix_norm),
        ffn2_norm=rows(ffn2_norm), ffn2_w_gate=bf(ffn2_w_gate), ffn2_w_up=bf(ffn2_w_up), ffn2_w_down=bf(ffn2_w_down),
        pool_w=bf(pool_w), pool_scale=row(pool_scale),
        rg_w_gate=bf(rg_w_gate), rg_w_in=bf(rg_w_in), rg_conv_w=rg_conv_w, rg_conv_b=row(rg_conv_b),
        rg_wa=bf(rg_wa), rg_ba=row(rg_ba), rg_wx=bf(rg_wx), rg_bx=row(rg_bx), rg_lambda=row(rg_lambda),
        rg_w_out=bf(rg_w_out),
        cv_w_pw1=bf(cv_w_pw1), cv_b_pw1=row(cv_b_pw1), cv_dw_w=cv_dw_w, cv_dw_b=row(cv_dw_b), cv_ln_g=row(cv_ln_g),
        cv_ln_b=row(cv_ln_b), cv_w_pw2=bf(cv_w_pw2), cv_b_pw2=row(cv_b_pw2),
        at_w_qkv=bf(at_w_qkv), at_q_norm=row(jnp.tile(at_q_norm, 2)), at_k_norm=row(jnp.tile(at_k_norm, 2)),
        at_w_o=bf(at_w_o),
        seg=((seg_id[:, None] == seg_id[None, :]).astype(F32) / HEAD_DIM).astype(BF16),
    )
    subln = row(at_subln)
    lam = pl.pallas_call(
        _lam_kernel,
        in_specs=[_const_spec((1, HEAD_DIM))] * 4,
        out_specs=_const_spec((1, 1)),
        out_shape=jax.ShapeDtypeStruct((1, 1), F32),
        name="diff_lambda",
    )(row(at_lam_q1), row(at_lam_k1), row(at_lam_q2), row(at_lam_k2))

    zeros = lambda n: jnp.zeros((bp, n, d), F32)
    prompt = dict(x=x_prompt, stride=1, pos0=0,
                  state=dict(pool=zeros(POOL_BUF), rgc=zeros(RG_CONV - 1), rgh=zeros(1), cv=zeros(CV_WIDTH - 1)))

    def to_tm(a):
        return jnp.swapaxes(a, 0, 1).reshape(1, a.shape[1] * bs, d)

    def from_tm(a):
        return jnp.swapaxes(a.reshape(-1, bs, d), 0, 1)

    def attend(qkvs):
        (qp, _, _, kbp, vtp), (qs, ks, vs) = qkvs
        q5 = from_tm(qs).reshape(bs, ts, nh, 2, HEAD_DIM)
        eye = jnp.eye(2, dtype=qs.dtype)
        qm = jnp.einsum('bqhcd,ce->bhcqed', q5, eye).reshape(bs, nh * 2 * ts, HEAD_WIDTH)
        k_new = from_tm(ks).reshape(bs, ts, nh, HEAD_WIDTH)
        v_new = from_tm(vs).reshape(bs, ts, nh, HEAD_WIDTH)
        shape = (bp, tp, d)
        heads_p, heads_s = _attention(page_table, lam, subln, qp.reshape(shape), kbp.reshape(shape), vtp, qm, k_new,
                                      v_new, cache_k, cache_v)
        return [heads_p.reshape(-1, d), jnp.transpose(heads_s, (2, 0, 1, 3)).reshape(ts * bs, d)]

    sample = dict(x=to_tm(x_sample), stride=bs, pos0=past,
                  state=dict(pool=to_tm(state_pool), rgc=to_tm(state_rglru_conv),
                             rgh=state_rglru_h.reshape(1, bs, d), cv=to_tm(state_conv)))
    (y_p, y_s), (o_p, o_s) = _trunk([prompt, sample], attend, p)

    kv4 = lambda a, b, t: a.reshape(b, t, nh, HEAD_WIDTH)
    return (y_p, from_tm(y_s),
            o_p['pool'], from_tm(o_s['pool']),
            o_p['rgc'], from_tm(o_s['rgc']),
            o_p['rgh'].reshape(bp, d), o_s['rgh'].reshape(bs, d),
            o_p['cv'], from_tm(o_s['cv']),
            kv4(o_p['k'], bp, tp), kv4(o_p['v'], bp, tp),
            kv4(from_tm(o_s['k']), bs, ts), kv4(from_tm(o_s['v']), bs, ts))
```

```python
import functools
import math

import jax
import jax.numpy as jnp
from jax import lax
from jax.experimental import pallas as pl
from jax.experimental.pallas import tpu as pltpu

F32 = jnp.float32
BF16 = jnp.bfloat16

NORM_EPS = 1e-6
POOL_WINDOWS = (2, 4, 8, 16)
POOL_BUF = max(POOL_WINDOWS) - 1
RG_CONV = 4
RG_C = 8.0
CV_WIDTH = 31
HEAD_DIM = 64
HEAD_WIDTH = 2 * HEAD_DIM
ROPE_THETA = 10000.0
ATT_LAYER = 3
LAMBDA_INIT = 0.8 - 0.6 * math.exp(-0.3 * ATT_LAYER)
NEG_INF = -1e30

Q_SCALE = HEAD_DIM ** -0.5 * math.log2(math.e)

SUBLANES = 8
VMEM_LIMIT = 56 * 1024 * 1024
PAGED_VMEM_BUDGET = 36 * 1024 * 1024
SEQ_TILE = 512
FFN_CHUNK = 256
FLASH_TILES_PER_STEP = 2
FLASH_UNROLL = 4
VT_ROWS = HEAD_WIDTH + 16


def _params(sem):
    return pltpu.CompilerParams(dimension_semantics=sem, vmem_limit_bytes=VMEM_LIMIT)


def _rms(x, g):
    ms = jnp.mean(x * x, axis=-1, keepdims=True)
    return x * lax.rsqrt(ms + NORM_EPS) * g


def _dot(a, b):
    return jnp.dot(a, b, preferred_element_type=F32)


def _dot_nt(a, b):
    return lax.dot_general(a, b, (((1,), (1,)), ((), ())), preferred_element_type=F32)


def _const_spec(shape):
    n = len(shape)
    return pl.BlockSpec(shape, lambda *_: (0,) * n)


def _ffn_kernel(*refs, nstreams, tiles, with_proj, nffn):
    per = 2 if with_proj else 1
    ins = refs[:nstreams * per]
    rest = refs[nstreams * per:]
    if with_proj:
        wo_ref, rest = rest[0], rest[1:]
    ffns = [rest[4 * n:4 * n + 4] for n in range(nffn)]
    outs = rest[4 * nffn:4 * nffn + nstreams]
    h_ref = rest[4 * nffn + nstreams]
    i = pl.program_id(0)
    start = 0
    for s in range(nstreams):
        @pl.when((i >= start) & (i < start + tiles[s]))
        def _(s=s):
            x = ins[s * per][...]
            if with_proj:
                x = x + _dot(ins[s * per + 1][...], wo_ref[...])
            rows = x.shape[0]
            for g_ref, wg_ref, wu_ref, wd_ref in ffns:
                u = _rms(x, g_ref[...]).astype(BF16)
                for c in range(wg_ref.shape[1] // FFN_CHUNK):
                    sl = slice(c * FFN_CHUNK, (c + 1) * FFN_CHUNK)
                    gate = _dot(u, wg_ref[:, sl])
                    up = _dot(u, wu_ref[:, sl])
                    h_ref[0:rows, sl] = (gate * jax.nn.sigmoid(gate) * up).astype(BF16)
                x = x + 0.5 * _dot(h_ref[0:rows, :], wd_ref[...])
            outs[s][...] = x
        start += tiles[s]


def _ffn(xs, ffns, proj=None):
    d = xs[0].shape[1]
    d_ff = ffns[0][2].shape[2]
    tms = [min(SEQ_TILE, x.shape[0]) for x in xs]
    tiles = [x.shape[0] // tm for x, tm in zip(xs, tms)]
    offsets = [sum(tiles[:s]) for s in range(len(xs))]
    single = pl.Buffered(1)

    def stream_spec(s, width):
        return pl.BlockSpec((tms[s], width),
                            lambda i, s=s: (jnp.clip(i - offsets[s], 0, tiles[s] - 1), 0))

    args, specs = [], []
    for s, x in enumerate(xs):
        args.append(x)
        specs.append(stream_spec(s, d))
        if proj is not None:
            args.append(proj[0][s])
            specs.append(stream_spec(s, proj[0][s].shape[1]))
    if proj is not None:
        args.append(proj[1])
        specs.append(pl.BlockSpec(proj[1].shape, lambda i: (0, 0), pipeline_mode=single))
    for layer, g, wg, wu, wd in ffns:
        pick = lambda i, layer=layer: (layer, 0, 0)
        args += [g, wg, wu, wd]
        specs += [pl.BlockSpec((None, 1, d), pick),
                  pl.BlockSpec((None, d, d_ff), pick, pipeline_mode=single),
                  pl.BlockSpec((None, d, d_ff), pick, pipeline_mode=single),
                  pl.BlockSpec((None, d_ff, d), pick, pipeline_mode=single)]
    return pl.pallas_call(
        functools.partial(_ffn_kernel, nstreams=len(xs), tiles=tuple(tiles), with_proj=proj is not None,
                          nffn=len(ffns)),
        grid=(sum(tiles),),
        in_specs=specs,
        out_specs=[stream_spec(s, d) for s in range(len(xs))],
        out_shape=[jax.ShapeDtypeStruct(x.shape, F32) for x in xs],
        scratch_shapes=[pltpu.VMEM((max(tms), d_ff), BF16)],
        compiler_params=_params(("arbitrary",)),
        name="ffn_proj" if proj is not None else "ffn",
    )(*args)


def _load_history(ext_ref, prev_ref, halo_rows):
    pb = prev_ref.shape[0]
    if halo_rows > pb:
        ext_ref[0:halo_rows - pb, :] = jnp.zeros((halo_rows - pb, ext_ref.shape[1]), F32)
    ext_ref[halo_rows - pb:halo_rows, :] = prev_ref[...]


def _pool_kernel(x_ref, prev_ref, g_ref, w_ref, sc_ref, o_ref, st_ref, ext_ref, *, stride, tt, halo, pos0, nsteps):
    t = pl.program_id(1)
    rows, hr = tt * stride, halo * stride
    pb = prev_ref.shape[0]

    @pl.when(t == 0)
    def _():
        _load_history(ext_ref, prev_ref, hr)

    x = x_ref[...]
    u = _rms(x, g_ref[...])
    ext_ref[hr:hr + rows, :] = u
    step = lax.broadcasted_iota(jnp.int32, (rows, 1), 0) >> int(math.log2(stride))
    pos = pos0 + t * tt + step
    gw = x.shape[1] // len(POOL_WINDOWS)
    for gi, win in enumerate(POOL_WINDOWS):
        cs = slice(gi * gw, (gi + 1) * gw)
        s = u[:, cs]
        for k in range(1, win):
            s = s + ext_ref[hr - k * stride:hr - k * stride + rows, cs]
        cnt = jnp.minimum(pos + 1, win).astype(F32)
        diff = (s / cnt - u[:, cs]).astype(BF16)
        o_ref[:, cs] = x[:, cs] + _dot(diff, w_ref[gi]) * sc_ref[:, cs]

    @pl.when(t == nsteps - 1)
    def _():
        st_ref[...] = ext_ref[hr + rows - pb:hr + rows, :]

    if nsteps > 1:
        ext_ref[0:hr, :] = ext_ref[rows:rows + hr, :]


def _seq_specs(rows, d):
    return pl.BlockSpec((None, rows, d), lambda g, t: (g, t, 0))


def _state_spec(shape):
    return pl.BlockSpec((None,) + tuple(shape[1:]), lambda g, t: (g,) + (0,) * (len(shape) - 1))


def _tiling(x, stride):
    groups, r, d = x.shape
    tt = min(SEQ_TILE, r) if stride == 1 else r // stride
    rows = tt * stride
    return groups, r, d, tt, rows, r // rows


def _pool_mixer(x, prev, g, w, scale, stride, pos0):
    groups, r, d, tt, rows, nsteps = _tiling(x, stride)
    halo = 16 if stride == 1 else POOL_BUF
    return pl.pallas_call(
        functools.partial(_pool_kernel, stride=stride, tt=tt, halo=halo, pos0=pos0, nsteps=nsteps),
        grid=(groups, nsteps),
        in_specs=[_seq_specs(rows, d), _state_spec(prev.shape), _const_spec(g.shape), _const_spec(w.shape),
                  _const_spec(scale.shape)],
        out_specs=[_seq_specs(rows, d), _state_spec(prev.shape)],
        out_shape=[jax.ShapeDtypeStruct(x.shape, F32), jax.ShapeDtypeStruct(prev.shape, F32)],
        scratch_shapes=[pltpu.VMEM((halo * stride + rows, d), F32)],
        compiler_params=_params(("arbitrary", "arbitrary")),
        name="pool_mixer",
    )(x, prev, g, w, scale)


def _rglru_kernel(x_ref, cprev_ref, hprev_ref, g_ref, wgate_ref, win_ref, cw_ref, cb_ref, wa_ref, ba_ref, wx_ref,
                  bx_ref, lam_ref, wout_ref, o_ref, cst_ref, hst_ref, ext_ref, h_ref, a_ref, b_ref,
                  *, stride, tt, halo, nsteps):
    t = pl.program_id(1)
    rows, hr = tt * stride, halo * stride
    pb = cprev_ref.shape[0]

    @pl.when(t == 0)
    def _():
        _load_history(ext_ref, cprev_ref, hr)
        h_ref[...] = hprev_ref[...]

    x = x_ref[...]
    u = _rms(x, g_ref[...]).astype(BF16)
    gate = jax.nn.gelu(_dot(u, wgate_ref[...]))
    zin = _dot(u, win_ref[...])
    ext_ref[hr:hr + rows, :] = zin
    z = cb_ref[...] + cw_ref[RG_CONV - 1:RG_CONV, :] * zin
    for k in range(RG_CONV - 1):
        back = (RG_CONV - 1 - k) * stride
        z = z + cw_ref[k:k + 1, :] * ext_ref[hr - back:hr - back + rows, :]
    zb = z.astype(BF16)
    lam = lam_ref[...]
    softplus_neg_lam = jnp.maximum(-lam, 0.0) + jnp.log1p(jnp.exp(-jnp.abs(lam)))
    nblk = wa_ref.shape[0]
    bw = x.shape[1] // nblk
    for n in range(nblk):
        cs = slice(n * bw, (n + 1) * bw)
        r = jax.nn.sigmoid(_dot(zb[:, cs], wa_ref[n]) + ba_ref[:, cs])
        i = jax.nn.sigmoid(_dot(zb[:, cs], wx_ref[n]) + bx_ref[:, cs])
        log_a = -RG_C * r * softplus_neg_lam[:, cs]
        a = jnp.exp(log_a)
        a_ref[:, cs] = a
        b_ref[:, cs] = jnp.sqrt(1.0 - a * a) * (i * z[:, cs])

    if stride == 1:
        def chunk(c, h):
            r0 = pl.multiple_of(c * SUBLANES, SUBLANES)
            a = a_ref[pl.ds(r0, SUBLANES), :]
            b = b_ref[pl.ds(r0, SUBLANES), :]
            row = lax.broadcasted_iota(jnp.int32, a.shape, 0)
            for s in (1, 2, 4):
                keep = row >= s
                b = jnp.where(keep, a * pltpu.roll(b, s, axis=0) + b, b)
                a = jnp.where(keep, a * pltpu.roll(a, s, axis=0), a)
            hs = a * h + b
            b_ref[pl.ds(r0, SUBLANES), :] = hs
            return hs[SUBLANES - 1:SUBLANES, :]

        h_ref[...] = lax.fori_loop(0, rows // SUBLANES, chunk, h_ref[...], unroll=4)
    else:
        h = h_ref[...]
        for s in range(tt):
            sl = slice(s * stride, (s + 1) * stride)
            h = a_ref[sl, :] * h + b_ref[sl, :]
            b_ref[sl, :] = h
        h_ref[...] = h

    y = _dot((gate * b_ref[...]).astype(BF16), wout_ref[...])
    o_ref[...] = x + y

    @pl.when(t == nsteps - 1)
    def _():
        cst_ref[...] = ext_ref[hr + rows - pb:hr + rows, :]
        hst_ref[...] = h_ref[...]

    if nsteps > 1:
        ext_ref[0:hr, :] = ext_ref[rows:rows + hr, :]


def _rglru_mixer(x, cprev, hprev, g, wgate, win, cw, cb, wa, ba, wx, bx, lam, wout, stride):
    groups, r, d, tt, rows, nsteps = _tiling(x, stride)
    halo = SUBLANES if stride == 1 else RG_CONV - 1
    consts = [g, wgate, win, cw, cb, wa, ba, wx, bx, lam, wout]
    return pl.pallas_call(
        functools.partial(_rglru_kernel, stride=stride, tt=tt, halo=halo, nsteps=nsteps),
        grid=(groups, nsteps),
        in_specs=[_seq_specs(rows, d), _state_spec(cprev.shape), _state_spec(hprev.shape)]
                 + [_const_spec(c.shape) for c in consts],
        out_specs=[_seq_specs(rows, d), _state_spec(cprev.shape), _state_spec(hprev.shape)],
        out_shape=[jax.ShapeDtypeStruct(x.shape, F32), jax.ShapeDtypeStruct(cprev.shape, F32),
                   jax.ShapeDtypeStruct(hprev.shape, F32)],
        scratch_shapes=[pltpu.VMEM((halo * stride + rows, d), F32), pltpu.VMEM((stride, d), F32),
                        pltpu.VMEM((rows, d), F32), pltpu.VMEM((rows, d), F32)],
        compiler_params=_params(("arbitrary", "arbitrary")),
        name="rglru_mixer",
    )(x, cprev, hprev, *consts)


def _convmod_kernel(x_ref, prev_ref, g_ref, w1_ref, b1_ref, dw_ref, dwb_ref, lng_ref, lnb_ref, w2_ref, b2_ref,
                    o_ref, st_ref, ext_ref, *, stride, tt, halo, nsteps):
    t = pl.program_id(1)
    rows, hr = tt * stride, halo * stride
    pb = prev_ref.shape[0]
    d = x_ref.shape[1]

    @pl.when(t == 0)
    def _():
        _load_history(ext_ref, prev_ref, hr)

    x = x_ref[...]
    u = _rms(x, g_ref[...]).astype(BF16)
    hc = _dot(u, w1_ref[...]) + b1_ref[...]
    glu = hc[:, :d] * jax.nn.sigmoid(hc[:, d:])
    ext_ref[hr:hr + rows, :] = glu
    c = dwb_ref[...]
    if stride == 1:
        for b in range(SUBLANES):
            phase = None
            for a in range((CV_WIDTH - 1 - b) // SUBLANES + 1):
                k = CV_WIDTH - 1 - (SUBLANES * a + b)
                start = hr - SUBLANES * (a + 1)
                term = dw_ref[k:k + 1, :] * ext_ref[start:start + rows + SUBLANES, :]
                phase = term if phase is None else phase + term
            c = c + phase[SUBLANES - b:SUBLANES - b + rows, :]
    else:
        for k in range(CV_WIDTH):
            back = (CV_WIDTH - 1 - k) * stride
            c = c + dw_ref[k:k + 1, :] * ext_ref[hr - back:hr - back + rows, :]
    cc = c - jnp.mean(c, axis=-1, keepdims=True)
    y = cc * lax.rsqrt(jnp.mean(cc * cc, axis=-1, keepdims=True) + NORM_EPS)
    y = y * lng_ref[...] + lnb_ref[...]
    y = (y * jax.nn.sigmoid(y)).astype(BF16)
    o_ref[...] = x + _dot(y, w2_ref[...]) + b2_ref[...]

    @pl.when(t == nsteps - 1)
    def _():
        st_ref[...] = ext_ref[hr + rows - pb:hr + rows, :]

    if nsteps > 1:
        ext_ref[0:hr, :] = ext_ref[rows:rows + hr, :]


def _conv_mixer(x, prev, g, w1, b1, dw, dwb, lng, lnb, w2, b2, stride):
    groups, r, d, tt, rows, nsteps = _tiling(x, stride)
    halo = 32 if stride == 1 else CV_WIDTH - 1
    consts = [g, w1, b1, dw, dwb, lng, lnb, w2, b2]
    return pl.pallas_call(
        functools.partial(_convmod_kernel, stride=stride, tt=tt, halo=halo, nsteps=nsteps),
        grid=(groups, nsteps),
        in_specs=[_seq_specs(rows, d), _state_spec(prev.shape)] + [_const_spec(c.shape) for c in consts],
        out_specs=[_seq_specs(rows, d), _state_spec(prev.shape)],
        out_shape=[jax.ShapeDtypeStruct(x.shape, F32), jax.ShapeDtypeStruct(prev.shape, F32)],
        scratch_shapes=[pltpu.VMEM((halo * stride + rows, d), F32)],
        compiler_params=_params(("arbitrary", "arbitrary")),
        name="conv_mixer",
    )(x, prev, g, w1, b1, dw, dwb, lng, lnb, w2, b2)


def _qkv_kernel(x_ref, g_ref, w_ref, qg_ref, kg_ref, cos_ref, sin_ref, seg_ref, q_ref, k_ref, v_ref, *flash_refs):
    d = x_ref.shape[1]
    u = _rms(x_ref[...], g_ref[...]).astype(BF16)
    qkv = _dot(u, w_ref[...])
    cos, sin = cos_ref[...], sin_ref[...]
    seg = seg_ref[...]
    lane = lax.broadcasted_iota(jnp.int32, (1, HEAD_WIDTH), 1)
    first_half = (lane & (HEAD_DIM - 1)) < HEAD_DIM // 2

    def norm_rope(blk, gain):
        sq = blk * blk
        hi = sq.astype(BF16)
        lo = (sq - hi.astype(F32)).astype(BF16)
        ms = _dot(hi, seg) + _dot(lo, seg)
        y = blk * lax.rsqrt(ms + NORM_EPS) * gain
        partner = jnp.where(first_half, pltpu.roll(y, HEAD_WIDTH - HEAD_DIM // 2, axis=1),
                            pltpu.roll(y, HEAD_DIM // 2, axis=1))
        return y * cos + partner * sin

    for h in range(d // HEAD_WIDTH):
        cs = slice(h * HEAD_WIDTH, (h + 1) * HEAD_WIDTH)
        q = norm_rope(qkv[:, cs], qg_ref[...])
        q_ref[:, cs] = (q * Q_SCALE).astype(BF16)
        k = norm_rope(qkv[:, d + h * HEAD_WIDTH:d + (h + 1) * HEAD_WIDTH], kg_ref[...])
        k_ref[:, cs] = k
        v = qkv[:, 2 * d + h * HEAD_WIDTH:2 * d + (h + 1) * HEAD_WIDTH]
        v_ref[:, cs] = v
        if flash_refs:
            kb_ref, vt_ref = flash_refs
            kb_ref[:, cs] = k.astype(BF16)
            vt_ref[h, 0:HEAD_WIDTH, :] = v.T.astype(BF16)
            pad = lax.broadcasted_iota(jnp.int32, (VT_ROWS - HEAD_WIDTH, v.shape[0]), 0)
            vt_ref[h, HEAD_WIDTH:VT_ROWS, :] = (pad == 0).astype(BF16)


def _qkv(x, g, w, qg, kg, cos, sin, seg, for_flash):
    n, d = x.shape
    tm = min(SEQ_TILE, n)
    nblk = cos.shape[0] // tm
    nh = d // HEAD_WIDTH
    row = pl.BlockSpec((tm, d), lambda i: (i, 0))
    tab = pl.BlockSpec((tm, HEAD_WIDTH), lambda i: (i % nblk, 0))
    out_specs = [row] * 3
    out_shape = [jax.ShapeDtypeStruct((n, d), dt) for dt in (BF16, F32, F32)]
    if for_flash:
        out_specs += [row, pl.BlockSpec((None, nh, None, VT_ROWS, tm), lambda i: (i // nblk, 0, i % nblk, 0, 0))]
        out_shape += [jax.ShapeDtypeStruct((n, d), BF16),
                      jax.ShapeDtypeStruct((n // (nblk * tm), nh, nblk, VT_ROWS, tm), BF16)]
    return pl.pallas_call(
        _qkv_kernel,
        grid=(n // tm,),
        in_specs=[row, _const_spec(g.shape), _const_spec(w.shape), _const_spec(qg.shape), _const_spec(kg.shape),
                  tab, tab, _const_spec(seg.shape)],
        out_specs=out_specs,
        out_shape=out_shape,
        compiler_params=_params(("arbitrary",)),
        name="qkv_rope",
    )(x, g, w, qg, kg, cos, sin, seg)


def _finish_heads(o, subln):
    ms = jnp.mean(o * o, axis=-1, keepdims=True)
    return o * lax.rsqrt(ms + NORM_EPS) * subln * (1.0 - LAMBDA_INIT)


def _flash_body(i, lam_ref, q_ref, k_ref, vt_ref, sub_ref, o_ref, acc_ref, sa_ref, sb_ref, m_ref, *, tile):
    q = q_ref[...]
    lane = lax.broadcasted_iota(jnp.int32, q.shape, 1)
    qh = (jnp.where(lane < HEAD_DIM, q, jnp.zeros_like(q)), jnp.where(lane >= HEAD_DIM, q, jnp.zeros_like(q)))
    acc_ref[...] = jnp.zeros(acc_ref.shape, F32)
    m_ref[...] = jnp.full(m_ref.shape, NEG_INF, F32)

    def produce(j, s_ref, c):
        k = k_ref[pl.ds(pl.multiple_of(j * tile, tile), tile), :]
        s_ref[c] = _dot_nt(k, qh[c])

    def consume(j, s_ref, c, masked):
        st = s_ref[c]
        if masked:
            visible = (lax.broadcasted_iota(jnp.int32, (tile, tile), 0)
                       <= lax.broadcasted_iota(jnp.int32, (tile, tile), 1))
            st = jnp.where(visible, st, NEG_INF)
        m_old = m_ref[c]
        m_new = jnp.maximum(m_old, jnp.max(st, axis=0, keepdims=True))
        pt = jnp.exp2(st - m_new).astype(BF16)
        acc_ref[c] = jnp.exp2(m_old - m_new) * acc_ref[c] + _dot(vt_ref[j], pt)
        m_ref[c] = m_new

    bufs = (sa_ref, sb_ref)

    def steps(j0, n, slot):
        for u in range(n):
            for c in range(2):
                produce(j0 + u + 1, bufs[(slot + u + 1) % 2], c)
                consume(j0 + u, bufs[(slot + u) % 2], c, False)

    for c in range(2):
        produce(0, sa_ref, c)

    def group(g, carry):
        steps(g * FLASH_UNROLL, FLASH_UNROLL, 0)
        return carry

    lax.fori_loop(0, i // FLASH_UNROLL, group, 0)
    for r in range(FLASH_UNROLL):
        @pl.when(i % FLASH_UNROLL == r)
        def _(r=r):
            steps(i - r, r, 0)
            for c in range(2):
                consume(i, bufs[r % 2], c, True)

    a0, a1 = acc_ref[0], acc_ref[1]
    ot = (a0[:HEAD_WIDTH] * (1.0 / a0[HEAD_WIDTH:HEAD_WIDTH + 1])
          - a1[:HEAD_WIDTH] * (lam_ref[...] / a1[HEAD_WIDTH:HEAD_WIDTH + 1]))
    ms = jnp.mean(ot * ot, axis=0, keepdims=True)
    ot = ot * lax.rsqrt(ms + NORM_EPS) * sub_ref[...] * (1.0 - LAMBDA_INIT)
    o_ref[...] = ot.T.astype(BF16)


def _lane_groups(x):
    return [x[:, g * HEAD_WIDTH:(g + 1) * HEAD_WIDTH] for g in range(x.shape[1] // HEAD_WIDTH)]


def _paged_body(lam_ref, qm_ref, knew_ref, vnew_ref, sub_ref, bias_ref, nbias_ref, kp_refs, vp_refs, o_ref, s_ref,
                *, nq, nh):
    npages = len(kp_refs)
    qm = qm_ref[...]
    flat = lambda ref: ref[...].reshape(-1, HEAD_WIDTH).astype(BF16)

    kn = flat(knew_ref)
    s_new = _dot_nt(qm, kn) + nbias_ref[...]
    top = None
    bias = bias_ref[...]
    for p in range(npages):
        groups = [g + bias for g in _lane_groups(_dot_nt(qm, flat(kp_refs[p])))]
        s_ref[p] = jnp.concatenate(groups, axis=1)
        page_top = functools.reduce(jnp.maximum, groups)
        top = page_top if top is None else jnp.maximum(top, page_top)
    m = jnp.maximum(jnp.max(top, axis=1, keepdims=True), jnp.max(s_new, axis=1, keepdims=True))
    m_lanes = jnp.broadcast_to(m, (m.shape[0], HEAD_WIDTH))

    p_new = jnp.exp2(s_new - m)
    acc = _dot(p_new.astype(BF16), flat(vnew_ref))
    lsum = None
    for p in range(npages):
        groups = [jnp.exp2(g - m_lanes) for g in _lane_groups(s_ref[p])]
        part = functools.reduce(jnp.add, groups)
        lsum = part if lsum is None else lsum + part
        acc = acc + _dot(jnp.concatenate(groups, axis=1).astype(BF16), flat(vp_refs[p]))
    l = jnp.sum(lsum, axis=1, keepdims=True) + jnp.sum(p_new, axis=1, keepdims=True)
    n = (acc / l).reshape(nh, 2 * nq, HEAD_WIDTH)
    o = n[:, 0:nq, :] - lam_ref[...] * n[:, nq:2 * nq, :]
    o_ref[...] = _finish_heads(o, sub_ref[...]).astype(BF16)


def _attention_kernel(pt_ref, lam_ref, q_ref, k_ref, vt_ref, subc_ref, qm_ref, knew_ref, vnew_ref, subr_ref, bias_ref,
                      nbias_ref, *refs, tile, npages, nq, nh, nseq):
    kp_refs, vp_refs = refs[:npages], refs[npages:2 * npages]
    o_ref, po_ref, acc_ref, sa_ref, sb_ref, m_ref, ps_ref = refs[2 * npages:]
    step = (pl.program_id(0) * pl.num_programs(1) + pl.program_id(1)) * pl.num_programs(2) + pl.program_id(2)

    @pl.when(step < nseq)
    def _():
        _paged_body(lam_ref, qm_ref, knew_ref, vnew_ref, subr_ref, bias_ref, nbias_ref, kp_refs, vp_refs, po_ref,
                    ps_ref, nq=nq, nh=nh)

    per_step = q_ref.shape[0] // tile
    for u in range(per_step):
        rows = pl.ds(u * tile, tile)
        _flash_body(pl.program_id(2) * per_step + u, lam_ref, q_ref.at[rows], k_ref, vt_ref, subc_ref, o_ref.at[rows],
                    acc_ref, sa_ref, sb_ref, m_ref, tile=tile)


def _attention(page_table, lam, subln, q, k, vt, qm, k_new, v_new, cache_k, cache_v):
    b, t, d = q.shape
    nblk, vrows, tile = vt.shape[2:]
    bs, npages = page_table.shape
    _, nq, nh, _ = k_new.shape
    page_keys = cache_k.shape[1]
    nrow = qm.shape[1]
    page_bytes = page_keys * nh * HEAD_WIDTH * 4
    assert 2 * 2 * npages * page_bytes <= PAGED_VMEM_BUDGET, "all pages of a sequence must fit one grid step"
    per_step = FLASH_TILES_PER_STEP if nblk % FLASH_TILES_PER_STEP == 0 else 1
    nsteps = nblk // per_step
    assert bs <= b * nh * nsteps, "one paged sequence per grid step"

    row_head = jnp.arange(nrow)[:, None] // (2 * nq)
    row_query = jnp.arange(nrow)[:, None] % nq
    assert HEAD_WIDTH % nh == 0 and (page_keys * nh) % HEAD_WIDTH == 0
    col = jnp.arange(HEAD_WIDTH)[None, :]
    bias = jnp.where(col % nh == row_head, 0.0, NEG_INF).astype(F32)
    ncol = jnp.arange(nq * nh)[None, :]
    nbias = jnp.where((ncol % nh == row_head) & (ncol // nh <= row_query), 0.0, NEG_INF).astype(F32)

    seq = lambda bi, h, i: jnp.minimum((bi * nh + h) * nsteps + i, bs - 1)
    page_block = (None,) + cache_k.shape[1:]
    page_spec = lambda p: pl.BlockSpec(page_block, lambda bi, h, i, pt: (pt[seq(bi, h, i), p], 0, 0, 0))
    seq4 = lambda shape: pl.BlockSpec((None,) + shape[1:], lambda bi, h, i, pt: (seq(bi, h, i), 0, 0, 0))
    const2 = lambda shape: pl.BlockSpec(shape, lambda bi, h, i, pt: (0, 0))
    grid_spec = pltpu.PrefetchScalarGridSpec(
        num_scalar_prefetch=1,
        grid=(b, nh, nsteps),
        in_specs=[const2((1, 1)),
                  pl.BlockSpec((None, per_step * tile, HEAD_WIDTH), lambda bi, h, i, pt: (bi, i, h)),
                  pl.BlockSpec((None, t, HEAD_WIDTH), lambda bi, h, i, pt: (bi, 0, h)),
                  pl.BlockSpec((None, None, nblk, vrows, tile), lambda bi, h, i, pt: (bi, h, 0, 0, 0)),
                  const2((HEAD_WIDTH, 1)),
                  pl.BlockSpec((None,) + qm.shape[1:], lambda bi, h, i, pt: (seq(bi, h, i), 0, 0)),
                  seq4(k_new.shape), seq4(v_new.shape), const2((1, HEAD_WIDTH)), const2(bias.shape),
                  const2(nbias.shape)]
                 + [page_spec(p) for p in range(npages)] * 2,
        out_specs=[pl.BlockSpec((None, per_step * tile, HEAD_WIDTH), lambda bi, h, i, pt: (bi, i, h)),
                   pl.BlockSpec((None, nh, nq, HEAD_WIDTH), lambda bi, h, i, pt: (seq(bi, h, i), 0, 0, 0))],
        scratch_shapes=[pltpu.VMEM((2, vrows, tile), F32), pltpu.VMEM((2, tile, tile), F32),
                        pltpu.VMEM((2, tile, tile), F32), pltpu.VMEM((2, 1, tile), F32),
                        pltpu.VMEM((npages, nrow, page_keys * nh), F32)],
    )
    return pl.pallas_call(
        functools.partial(_attention_kernel, tile=tile, npages=npages, nq=nq, nh=nh, nseq=bs),
        grid_spec=grid_spec,
        out_shape=[jax.ShapeDtypeStruct((b, t, d), BF16), jax.ShapeDtypeStruct((bs, nh, nq, HEAD_WIDTH), BF16)],
        compiler_params=_params(("arbitrary", "arbitrary", "arbitrary")),
        name="diff_attention",
    )(page_table, lam, q, k, vt, subln.reshape(-1, 1), qm, k_new, v_new, subln, bias, nbias,
      *([cache_k] * npages), *([cache_v] * npages))


def _lam_kernel(q1_ref, k1_ref, q2_ref, k2_ref, o_ref):
    dot1 = jnp.sum(q1_ref[...] * k1_ref[...], axis=-1, keepdims=True)
    dot2 = jnp.sum(q2_ref[...] * k2_ref[...], axis=-1, keepdims=True)
    o_ref[...] = jnp.exp(dot1) - jnp.exp(dot2) + LAMBDA_INIT


def _rope_tables(pos):
    half = HEAD_DIM // 2
    inv = ROPE_THETA ** (-jnp.arange(half, dtype=F32) / half)
    ang = pos.astype(F32)[:, None] * inv[None, :]
    cos, sin = jnp.cos(ang), jnp.sin(ang)
    return jnp.tile(cos, (1, 4)), jnp.tile(jnp.concatenate([-sin, sin], axis=1), (1, 2))


def _mixer(i, x, stride, pos0, st, p):
    groups, r, d = x.shape
    g = p['mix_norm'][i]
    if i == 0:
        x, pool = _pool_mixer(x, st['pool'], g, p['pool_w'], p['pool_scale'], stride, pos0)
        return x, dict(pool=pool), None
    if i == 1:
        x, rgc, rgh = _rglru_mixer(
            x, st['rgc'], st['rgh'], g, p['rg_w_gate'], p['rg_w_in'], p['rg_conv_w'], p['rg_conv_b'], p['rg_wa'],
            p['rg_ba'], p['rg_wx'], p['rg_bx'], p['rg_lambda'], p['rg_w_out'], stride)
        return x, dict(rgc=rgc, rgh=rgh), None
    if i == 2:
        x, cv = _conv_mixer(x, st['cv'], g, p['cv_w_pw1'], p['cv_b_pw1'], p['cv_dw_w'], p['cv_dw_b'],
                            p['cv_ln_g'], p['cv_ln_b'], p['cv_w_pw2'], p['cv_b_pw2'], stride)
        return x, dict(cv=cv), None
    pos = pos0 + jnp.arange(r, dtype=jnp.int32) // stride
    cos, sin = _rope_tables(pos)
    q, k, v, *flash_kv = _qkv(x.reshape(groups * r, d), g, p['at_w_qkv'], p['at_q_norm'], p['at_k_norm'], cos, sin,
                              p['seg'], for_flash=stride == 1)
    return x, dict(k=k, v=v), (q, k, v, *flash_kv)


def _trunk(groups, attend, p):
    xs = [grp['x'] for grp in groups]
    outs = [dict() for _ in groups]
    nlayers = p['mix_norm'].shape[0]

    def ffn_weights(which, i):
        return (i,) + tuple(p[f'{which}_{name}'] for name in ('norm', 'w_gate', 'w_up', 'w_down'))

    def run_ffn(xs, ffns, proj=None):
        ys = _ffn([x.reshape(-1, x.shape[-1]) for x in xs], ffns, proj=proj)
        return [y.reshape(x.shape) for x, y in zip(xs, ys)]

    xs = run_ffn(xs, [ffn_weights('ffn1', 0)])
    for i in range(nlayers):
        qkvs = []
        for n, grp in enumerate(groups):
            xs[n], new_state, qkv = _mixer(i, xs[n], grp['stride'], grp['pos0'], grp['state'], p)
            outs[n].update(new_state)
            qkvs.append(qkv)
        proj = (attend(qkvs), p['at_w_o']) if qkvs[0] is not None else None
        ffns = [ffn_weights('ffn2', i)] + ([ffn_weights('ffn1', i + 1)] if i + 1 < nlayers else [])
        xs = run_ffn(xs, ffns, proj=proj)
    return xs, outs


def kernel(x_prompt, x_sample, state_pool, state_rglru_conv, state_rglru_h, state_conv, cache_k, cache_v, page_table, ffn1_norm, ffn1_w_gate, ffn1_w_up, ffn1_w_down, mix_norm, ffn2_norm, ffn2_w_gate, ffn2_w_up, ffn2_w_down, pool_w, pool_scale, rg_w_gate, rg_w_in, rg_conv_w, rg_conv_b, rg_wa, rg_ba, rg_wx, rg_bx, rg_lambda, rg_w_out, cv_w_pw1, cv_b_pw1, cv_dw_w, cv_dw_b, cv_ln_g, cv_ln_b, cv_w_pw2, cv_b_pw2, at_w_qkv, at_q_norm, at_k_norm, at_lam_q1, at_lam_k1, at_lam_q2, at_lam_k2, at_subln, at_w_o):
    bp, tp, d = x_prompt.shape
    bs, ts, _ = x_sample.shape
    nh = d // HEAD_WIDTH
    past = page_table.shape[1] * cache_k.shape[1]
    bf = lambda w: w.astype(BF16)
    row = lambda v: v.reshape(1, -1).astype(F32)
    rows = lambda v: v.reshape(v.shape[0], 1, -1).astype(F32)

    seg_id = jnp.arange(HEAD_WIDTH) // HEAD_DIM
    p = dict(
        ffn1_norm=rows(ffn1_norm), ffn1_w_gate=bf(ffn1_w_gate), ffn1_w_up=bf(ffn1_w_up), ffn1_w_down=bf(ffn1_w_down),
        mix_norm=rows(mix_norm),
        ffn2_norm=rows(ffn2_norm), ffn2_w_gate=bf(ffn2_w_gate), ffn2_w_up=bf(ffn2_w_up), ffn2_w_down=bf(ffn2_w_down),
        pool_w=bf(pool_w), pool_scale=row(pool_scale),
        rg_w_gate=bf(rg_w_gate), rg_w_in=bf(rg_w_in), rg_conv_w=rg_conv_w, rg_conv_b=row(rg_conv_b),
        rg_wa=bf(rg_wa), rg_ba=row(rg_ba), rg_wx=bf(rg_wx), rg_bx=row(rg_bx), rg_lambda=row(rg_lambda),
        rg_w_out=bf(rg_w_out),
        cv_w_pw1=bf(cv_w_pw1), cv_b_pw1=row(cv_b_pw1), cv_dw_w=cv_dw_w, cv_dw_b=row(cv_dw_b), cv_ln_g=row(cv_ln_g),
        cv_ln_b=row(cv_ln_b), cv_w_pw2=bf(cv_w_pw2), cv_b_pw2=row(cv_b_pw2),
        at_w_qkv=bf(at_w_qkv), at_q_norm=row(jnp.tile(at_q_norm, 2)), at_k_norm=row(jnp.tile(at_k_norm, 2)),
        at_w_o=bf(at_w_o),
        seg=((seg_id[:, None] == seg_id[None, :]).astype(F32) / HEAD_DIM).astype(BF16),
    )
    subln = row(at_subln)
    lam = pl.pallas_call(
        _lam_kernel,
        in_specs=[_const_spec((1, HEAD_DIM))] * 4,
        out_specs=_const_spec((1, 1)),
        out_shape=jax.ShapeDtypeStruct((1, 1), F32),
        name="diff_lambda",
    )(row(at_lam_q1), row(at_lam_k1), row(at_lam_q2), row(at_lam_k2))

    zeros = lambda n: jnp.zeros((bp, n, d), F32)
    prompt = dict(x=x_prompt, stride=1, pos0=0,
                  state=dict(pool=zeros(POOL_BUF), rgc=zeros(RG_CONV - 1), rgh=zeros(1), cv=zeros(CV_WIDTH - 1)))

    def to_tm(a):
        return jnp.swapaxes(a, 0, 1).reshape(1, a.shape[1] * bs, d)

    def from_tm(a):
        return jnp.swapaxes(a.reshape(-1, bs, d), 0, 1)

    def attend(qkvs):
        (qp, _, _, kbp, vtp), (qs, ks, vs) = qkvs
        q5 = from_tm(qs).reshape(bs, ts, nh, 2, HEAD_DIM)
        eye = jnp.eye(2, dtype=qs.dtype)
        qm = jnp.einsum('bqhcd,ce->bhcqed', q5, eye).reshape(bs, nh * 2 * ts, HEAD_WIDTH)
        k_new = from_tm(ks).reshape(bs, ts, nh, HEAD_WIDTH)
        v_new = from_tm(vs).reshape(bs, ts, nh, HEAD_WIDTH)
        shape = (bp, tp, d)
        heads_p, heads_s = _attention(page_table, lam, subln, qp.reshape(shape), kbp.reshape(shape), vtp, qm, k_new,
                                      v_new, cache_k, cache_v)
        return [heads_p.reshape(-1, d), jnp.transpose(heads_s, (2, 0, 1, 3)).reshape(ts * bs, d)]

    sample = dict(x=to_tm(x_sample), stride=bs, pos0=past,
                  state=dict(pool=to_tm(state_pool), rgc=to_tm(state_rglru_conv),
                             rgh=state_rglru_h.reshape(1, bs, d), cv=to_tm(state_conv)))
    (y_p, y_s), (o_p, o_s) = _trunk([prompt, sample], attend, p)

    kv4 = lambda a, b, t: a.reshape(b, t, nh, HEAD_WIDTH)
    return (y_p, from_tm(y_s),
            o_p['pool'], from_tm(o_s['pool']),
            o_p['rgc'], from_tm(o_s['rgc']),
            o_p['rgh'].reshape(bp, d), o_s['rgh'].reshape(bs, d),
            o_p['cv'], from_tm(o_s['cv']),
            kv4(o_p['k'], bp, tp), kv4(o_p['v'], bp, tp),
            kv4(from_tm(o_s['k']), bs, ts), kv4(from_tm(o_s['v']), bs, ts))
```

```python
import functools
import math

import jax
import jax.numpy as jnp
from jax import lax
from jax.experimental import pallas as pl
from jax.experimental.pallas import tpu as pltpu

F32 = jnp.float32
BF16 = jnp.bfloat16

NORM_EPS = 1e-6
POOL_WINDOWS = (2, 4, 8, 16)
POOL_BUF = max(POOL_WINDOWS) - 1
RG_CONV = 4
RG_C = 8.0
CV_WIDTH = 31
HEAD_DIM = 64
HEAD_WIDTH = 2 * HEAD_DIM
ROPE_THETA = 10000.0
ATT_LAYER = 3
LAMBDA_INIT = 0.8 - 0.6 * math.exp(-0.3 * ATT_LAYER)
NEG_INF = -1e30

Q_SCALE = HEAD_DIM ** -0.5 * math.log2(math.e)

SUBLANES = 8
VMEM_LIMIT = 56 * 1024 * 1024
PAGED_VMEM_BUDGET = 36 * 1024 * 1024
SEQ_TILE = 512
FFN_CHUNK = 256
FLASH_TILES_PER_STEP = 2
FLASH_UNROLL = 8
VT_ROWS = HEAD_WIDTH + 16


def _params(sem):
    return pltpu.CompilerParams(dimension_semantics=sem, vmem_limit_bytes=VMEM_LIMIT)


def _rms(x, g):
    ms = jnp.mean(x * x, axis=-1, keepdims=True)
    return x * lax.rsqrt(ms + NORM_EPS) * g


def _dot(a, b):
    return jnp.dot(a, b, preferred_element_type=F32)


def _dot_nt(a, b):
    return lax.dot_general(a, b, (((1,), (1,)), ((), ())), preferred_element_type=F32)


def _const_spec(shape):
    n = len(shape)
    return pl.BlockSpec(shape, lambda *_: (0,) * n)


def _ffn_kernel(*refs, nstreams, tiles, with_proj, nffn):
    per = 2 if with_proj else 1
    ins = refs[:nstreams * per]
    rest = refs[nstreams * per:]
    if with_proj:
        wo_ref, rest = rest[0], rest[1:]
    ffns = [rest[4 * n:4 * n + 4] for n in range(nffn)]
    outs = rest[4 * nffn:4 * nffn + nstreams]
    h_ref = rest[4 * nffn + nstreams]
    i = pl.program_id(0)
    start = 0
    for s in range(nstreams):
        @pl.when((i >= start) & (i < start + tiles[s]))
        def _(s=s):
            x = ins[s * per][...]
            if with_proj:
                x = x + _dot(ins[s * per + 1][...], wo_ref[...])
            rows = x.shape[0]
            for g_ref, wg_ref, wu_ref, wd_ref in ffns:
                u = _rms(x, g_ref[...]).astype(BF16)
                for c in range(wg_ref.shape[1] // FFN_CHUNK):
                    sl = slice(c * FFN_CHUNK, (c + 1) * FFN_CHUNK)
                    gate = _dot(u, wg_ref[:, sl])
                    up = _dot(u, wu_ref[:, sl])
                    h_ref[0:rows, sl] = (gate * jax.nn.sigmoid(gate) * up).astype(BF16)
                x = x + 0.5 * _dot(h_ref[0:rows, :], wd_ref[...])
            outs[s][...] = x
        start += tiles[s]


def _ffn(xs, ffns, proj=None):
    d = xs[0].shape[1]
    d_ff = ffns[0][2].shape[2]
    tms = [min(SEQ_TILE, x.shape[0]) for x in xs]
    tiles = [x.shape[0] // tm for x, tm in zip(xs, tms)]
    offsets = [sum(tiles[:s]) for s in range(len(xs))]
    single = pl.Buffered(1)

    def stream_spec(s, width):
        return pl.BlockSpec((tms[s], width),
                            lambda i, s=s: (jnp.clip(i - offsets[s], 0, tiles[s] - 1), 0))

    args, specs = [], []
    for s, x in enumerate(xs):
        args.append(x)
        specs.append(stream_spec(s, d))
        if proj is not None:
            args.append(proj[0][s])
            specs.append(stream_spec(s, proj[0][s].shape[1]))
    if proj is not None:
        args.append(proj[1])
        specs.append(pl.BlockSpec(proj[1].shape, lambda i: (0, 0), pipeline_mode=single))
    for layer, g, wg, wu, wd in ffns:
        pick = lambda i, layer=layer: (layer, 0, 0)
        args += [g, wg, wu, wd]
        specs += [pl.BlockSpec((None, 1, d), pick),
                  pl.BlockSpec((None, d, d_ff), pick, pipeline_mode=single),
                  pl.BlockSpec((None, d, d_ff), pick, pipeline_mode=single),
                  pl.BlockSpec((None, d_ff, d), pick, pipeline_mode=single)]
    return pl.pallas_call(
        functools.partial(_ffn_kernel, nstreams=len(xs), tiles=tuple(tiles), with_proj=proj is not None,
                          nffn=len(ffns)),
        grid=(sum(tiles),),
        in_specs=specs,
        out_specs=[stream_spec(s, d) for s in range(len(xs))],
        out_shape=[jax.ShapeDtypeStruct(x.shape, F32) for x in xs],
        scratch_shapes=[pltpu.VMEM((max(tms), d_ff), BF16)],
        compiler_params=_params(("arbitrary",)),
        name="ffn_proj" if proj is not None else "ffn",
    )(*args)


def _load_history(ext_ref, prev_ref, halo_rows):
    pb = prev_ref.shape[0]
    if halo_rows > pb:
        ext_ref[0:halo_rows - pb, :] = jnp.zeros((halo_rows - pb, ext_ref.shape[1]), F32)
    ext_ref[halo_rows - pb:halo_rows, :] = prev_ref[...]


def _pool_kernel(x_ref, prev_ref, g_ref, w_ref, sc_ref, o_ref, st_ref, ext_ref, *, stride, tt, halo, pos0, nsteps):
    t = pl.program_id(1)
    rows, hr = tt * stride, halo * stride
    pb = prev_ref.shape[0]

    @pl.when(t == 0)
    def _():
        _load_history(ext_ref, prev_ref, hr)

    x = x_ref[...]
    u = _rms(x, g_ref[...])
    ext_ref[hr:hr + rows, :] = u
    step = lax.broadcasted_iota(jnp.int32, (rows, 1), 0) >> int(math.log2(stride))
    pos = pos0 + t * tt + step
    gw = x.shape[1] // len(POOL_WINDOWS)
    for gi, win in enumerate(POOL_WINDOWS):
        cs = slice(gi * gw, (gi + 1) * gw)
        s = u[:, cs]
        for k in range(1, win):
            s = s + ext_ref[hr - k * stride:hr - k * stride + rows, cs]
        cnt = jnp.minimum(pos + 1, win).astype(F32)
        diff = (s / cnt - u[:, cs]).astype(BF16)
        o_ref[:, cs] = x[:, cs] + _dot(diff, w_ref[gi]) * sc_ref[:, cs]

    @pl.when(t == nsteps - 1)
    def _():
        st_ref[...] = ext_ref[hr + rows - pb:hr + rows, :]

    if nsteps > 1:
        ext_ref[0:hr, :] = ext_ref[rows:rows + hr, :]


def _seq_specs(rows, d):
    return pl.BlockSpec((None, rows, d), lambda g, t: (g, t, 0))


def _state_spec(shape):
    return pl.BlockSpec((None,) + tuple(shape[1:]), lambda g, t: (g,) + (0,) * (len(shape) - 1))


def _tiling(x, stride):
    groups, r, d = x.shape
    tt = min(SEQ_TILE, r) if stride == 1 else r // stride
    rows = tt * stride
    return groups, r, d, tt, rows, r // rows


def _pool_mixer(x, prev, g, w, scale, stride, pos0):
    groups, r, d, tt, rows, nsteps = _tiling(x, stride)
    halo = 16 if stride == 1 else POOL_BUF
    return pl.pallas_call(
        functools.partial(_pool_kernel, stride=stride, tt=tt, halo=halo, pos0=pos0, nsteps=nsteps),
        grid=(groups, nsteps),
        in_specs=[_seq_specs(rows, d), _state_spec(prev.shape), _const_spec(g.shape), _const_spec(w.shape),
                  _const_spec(scale.shape)],
        out_specs=[_seq_specs(rows, d), _state_spec(prev.shape)],
        out_shape=[jax.ShapeDtypeStruct(x.shape, F32), jax.ShapeDtypeStruct(prev.shape, F32)],
        scratch_shapes=[pltpu.VMEM((halo * stride + rows, d), F32)],
        compiler_params=_params(("arbitrary", "arbitrary")),
        name="pool_mixer",
    )(x, prev, g, w, scale)


def _rglru_kernel(x_ref, cprev_ref, hprev_ref, g_ref, wgate_ref, win_ref, cw_ref, cb_ref, wa_ref, ba_ref, wx_ref,
                  bx_ref, lam_ref, wout_ref, o_ref, cst_ref, hst_ref, ext_ref, h_ref, a_ref, b_ref,
                  *, stride, tt, halo, nsteps):
    t = pl.program_id(1)
    rows, hr = tt * stride, halo * stride
    pb = cprev_ref.shape[0]

    @pl.when(t == 0)
    def _():
        _load_history(ext_ref, cprev_ref, hr)
        h_ref[...] = hprev_ref[...]

    x = x_ref[...]
    u = _rms(x, g_ref[...]).astype(BF16)
    gate = jax.nn.gelu(_dot(u, wgate_ref[...]))
    zin = _dot(u, win_ref[...])
    ext_ref[hr:hr + rows, :] = zin
    z = cb_ref[...] + cw_ref[RG_CONV - 1:RG_CONV, :] * zin
    for k in range(RG_CONV - 1):
        back = (RG_CONV - 1 - k) * stride
        z = z + cw_ref[k:k + 1, :] * ext_ref[hr - back:hr - back + rows, :]
    zb = z.astype(BF16)
    lam = lam_ref[...]
    softplus_neg_lam = jnp.maximum(-lam, 0.0) + jnp.log1p(jnp.exp(-jnp.abs(lam)))
    nblk = wa_ref.shape[0]
    bw = x.shape[1] // nblk
    for n in range(nblk):
        cs = slice(n * bw, (n + 1) * bw)
        r = jax.nn.sigmoid(_dot(zb[:, cs], wa_ref[n]) + ba_ref[:, cs])
        i = jax.nn.sigmoid(_dot(zb[:, cs], wx_ref[n]) + bx_ref[:, cs])
        log_a = -RG_C * r * softplus_neg_lam[:, cs]
        a = jnp.exp(log_a)
        a_ref[:, cs] = a
        b_ref[:, cs] = jnp.sqrt(1.0 - a * a) * (i * z[:, cs])

    if stride == 1:
        def chunk(c, h):
            r0 = pl.multiple_of(c * SUBLANES, SUBLANES)
            a = a_ref[pl.ds(r0, SUBLANES), :]
            b = b_ref[pl.ds(r0, SUBLANES), :]
            row = lax.broadcasted_iota(jnp.int32, a.shape, 0)
            for s in (1, 2, 4):
                keep = row >= s
                b = jnp.where(keep, a * pltpu.roll(b, s, axis=0) + b, b)
                a = jnp.where(keep, a * pltpu.roll(a, s, axis=0), a)
            hs = a * h + b
            b_ref[pl.ds(r0, SUBLANES), :] = hs
            return hs[SUBLANES - 1:SUBLANES, :]

        h_ref[...] = lax.fori_loop(0, rows // SUBLANES, chunk, h_ref[...], unroll=4)
    else:
        h = h_ref[...]
        for s in range(tt):
            sl = slice(s * stride, (s + 1) * stride)
            h = a_ref[sl, :] * h + b_ref[sl, :]
            b_ref[sl, :] = h
        h_ref[...] = h

    y = _dot((gate * b_ref[...]).astype(BF16), wout_ref[...])
    o_ref[...] = x + y

    @pl.when(t == nsteps - 1)
    def _():
        cst_ref[...] = ext_ref[hr + rows - pb:hr + rows, :]
        hst_ref[...] = h_ref[...]

    if nsteps > 1:
        ext_ref[0:hr, :] = ext_ref[rows:rows + hr, :]


def _rglru_mixer(x, cprev, hprev, g, wgate, win, cw, cb, wa, ba, wx, bx, lam, wout, stride):
    groups, r, d, tt, rows, nsteps = _tiling(x, stride)
    halo = SUBLANES if stride == 1 else RG_CONV - 1
    consts = [g, wgate, win, cw, cb, wa, ba, wx, bx, lam, wout]
    return pl.pallas_call(
        functools.partial(_rglru_kernel, stride=stride, tt=tt, halo=halo, nsteps=nsteps),
        grid=(groups, nsteps),
        in_specs=[_seq_specs(rows, d), _state_spec(cprev.shape), _state_spec(hprev.shape)]
                 + [_const_spec(c.shape) for c in consts],
        out_specs=[_seq_specs(rows, d), _state_spec(cprev.shape), _state_spec(hprev.shape)],
        out_shape=[jax.ShapeDtypeStruct(x.shape, F32), jax.ShapeDtypeStruct(cprev.shape, F32),
                   jax.ShapeDtypeStruct(hprev.shape, F32)],
        scratch_shapes=[pltpu.VMEM((halo * stride + rows, d), F32), pltpu.VMEM((stride, d), F32),
                        pltpu.VMEM((rows, d), F32), pltpu.VMEM((rows, d), F32)],
        compiler_params=_params(("arbitrary", "arbitrary")),
        name="rglru_mixer",
    )(x, cprev, hprev, *consts)


def _convmod_kernel(x_ref, prev_ref, g_ref, w1_ref, b1_ref, dw_ref, dwb_ref, lng_ref, lnb_ref, w2_ref, b2_ref,
                    o_ref, st_ref, ext_ref, *, stride, tt, halo, nsteps):
    t = pl.program_id(1)
    rows, hr = tt * stride, halo * stride
    pb = prev_ref.shape[0]
    d = x_ref.shape[1]

    @pl.when(t == 0)
    def _():
        _load_history(ext_ref, prev_ref, hr)

    x = x_ref[...]
    u = _rms(x, g_ref[...]).astype(BF16)
    hc = _dot(u, w1_ref[...]) + b1_ref[...]
    glu = hc[:, :d] * jax.nn.sigmoid(hc[:, d:])
    ext_ref[hr:hr + rows, :] = glu
    c = dwb_ref[...]
    if stride == 1:
        for b in range(SUBLANES):
            phase = None
            for a in range((CV_WIDTH - 1 - b) // SUBLANES + 1):
                k = CV_WIDTH - 1 - (SUBLANES * a + b)
                start = hr - SUBLANES * (a + 1)
                term = dw_ref[k:k + 1, :] * ext_ref[start:start + rows + SUBLANES, :]
                phase = term if phase is None else phase + term
            c = c + phase[SUBLANES - b:SUBLANES - b + rows, :]
    else:
        for k in range(CV_WIDTH):
            back = (CV_WIDTH - 1 - k) * stride
            c = c + dw_ref[k:k + 1, :] * ext_ref[hr - back:hr - back + rows, :]
    cc = c - jnp.mean(c, axis=-1, keepdims=True)
    y = cc * lax.rsqrt(jnp.mean(cc * cc, axis=-1, keepdims=True) + NORM_EPS)
    y = y * lng_ref[...] + lnb_ref[...]
    y = (y * jax.nn.sigmoid(y)).astype(BF16)
    o_ref[...] = x + _dot(y, w2_ref[...]) + b2_ref[...]

    @pl.when(t == nsteps - 1)
    def _():
        st_ref[...] = ext_ref[hr + rows - pb:hr + rows, :]

    if nsteps > 1:
        ext_ref[0:hr, :] = ext_ref[rows:rows + hr, :]


def _conv_mixer(x, prev, g, w1, b1, dw, dwb, lng, lnb, w2, b2, stride):
    groups, r, d, tt, rows, nsteps = _tiling(x, stride)
    halo = 32 if stride == 1 else CV_WIDTH - 1
    consts = [g, w1, b1, dw, dwb, lng, lnb, w2, b2]
    return pl.pallas_call(
        functools.partial(_convmod_kernel, stride=stride, tt=tt, halo=halo, nsteps=nsteps),
        grid=(groups, nsteps),
        in_specs=[_seq_specs(rows, d), _state_spec(prev.shape)] + [_const_spec(c.shape) for c in consts],
        out_specs=[_seq_specs(rows, d), _state_spec(prev.shape)],
        out_shape=[jax.ShapeDtypeStruct(x.shape, F32), jax.ShapeDtypeStruct(prev.shape, F32)],
        scratch_shapes=[pltpu.VMEM((halo * stride + rows, d), F32)],
        compiler_params=_params(("arbitrary", "arbitrary")),
        name="conv_mixer",
    )(x, prev, g, w1, b1, dw, dwb, lng, lnb, w2, b2)


def _qkv_kernel(x_ref, g_ref, w_ref, qg_ref, kg_ref, cos_ref, sin_ref, seg_ref, q_ref, k_ref, v_ref, *flash_refs):
    d = x_ref.shape[1]
    u = _rms(x_ref[...], g_ref[...]).astype(BF16)
    qkv = _dot(u, w_ref[...])
    cos, sin = cos_ref[...], sin_ref[...]
    seg = seg_ref[...]
    lane = lax.broadcasted_iota(jnp.int32, (1, HEAD_WIDTH), 1)
    first_half = (lane & (HEAD_DIM - 1)) < HEAD_DIM // 2

    def norm_rope(blk, gain):
        sq = blk * blk
        hi = sq.astype(BF16)
        lo = (sq - hi.astype(F32)).astype(BF16)
        ms = _dot(hi, seg) + _dot(lo, seg)
        y = blk * lax.rsqrt(ms + NORM_EPS) * gain
        partner = jnp.where(first_half, pltpu.roll(y, HEAD_WIDTH - HEAD_DIM // 2, axis=1),
                            pltpu.roll(y, HEAD_DIM // 2, axis=1))
        return y * cos + partner * sin

    for h in range(d // HEAD_WIDTH):
        cs = slice(h * HEAD_WIDTH, (h + 1) * HEAD_WIDTH)
        q = norm_rope(qkv[:, cs], qg_ref[...])
        q_ref[:, cs] = (q * Q_SCALE).astype(BF16)
        k = norm_rope(qkv[:, d + h * HEAD_WIDTH:d + (h + 1) * HEAD_WIDTH], kg_ref[...])
        k_ref[:, cs] = k
        v = qkv[:, 2 * d + h * HEAD_WIDTH:2 * d + (h + 1) * HEAD_WIDTH]
        v_ref[:, cs] = v
        if flash_refs:
            kb_ref, vt_ref = flash_refs
            kb_ref[:, cs] = k.astype(BF16)
            vt_ref[h, 0:HEAD_WIDTH, :] = v.T.astype(BF16)
            pad = lax.broadcasted_iota(jnp.int32, (VT_ROWS - HEAD_WIDTH, v.shape[0]), 0)
            vt_ref[h, HEAD_WIDTH:VT_ROWS, :] = (pad == 0).astype(BF16)


def _qkv(x, g, w, qg, kg, cos, sin, seg, for_flash):
    n, d = x.shape
    tm = min(SEQ_TILE, n)
    nblk = cos.shape[0] // tm
    nh = d // HEAD_WIDTH
    row = pl.BlockSpec((tm, d), lambda i: (i, 0))
    tab = pl.BlockSpec((tm, HEAD_WIDTH), lambda i: (i % nblk, 0))
    out_specs = [row] * 3
    out_shape = [jax.ShapeDtypeStruct((n, d), dt) for dt in (BF16, F32, F32)]
    if for_flash:
        out_specs += [row, pl.BlockSpec((None, nh, None, VT_ROWS, tm), lambda i: (i // nblk, 0, i % nblk, 0, 0))]
        out_shape += [jax.ShapeDtypeStruct((n, d), BF16),
                      jax.ShapeDtypeStruct((n // (nblk * tm), nh, nblk, VT_ROWS, tm), BF16)]
    return pl.pallas_call(
        _qkv_kernel,
        grid=(n // tm,),
        in_specs=[row, _const_spec(g.shape), _const_spec(w.shape), _const_spec(qg.shape), _const_spec(kg.shape),
                  tab, tab, _const_spec(seg.shape)],
        out_specs=out_specs,
        out_shape=out_shape,
        compiler_params=_params(("arbitrary",)),
        name="qkv_rope",
    )(x, g, w, qg, kg, cos, sin, seg)


def _finish_heads(o, subln):
    ms = jnp.mean(o * o, axis=-1, keepdims=True)
    return o * lax.rsqrt(ms + NORM_EPS) * subln * (1.0 - LAMBDA_INIT)


def _flash_body(i, lam_ref, q_ref, k_ref, vt_ref, sub_ref, o_ref, acc_ref, sa_ref, sb_ref, m_ref, *, tile):
    q = q_ref[...]
    lane = lax.broadcasted_iota(jnp.int32, q.shape, 1)
    qh = (jnp.where(lane < HEAD_DIM, q, jnp.zeros_like(q)), jnp.where(lane >= HEAD_DIM, q, jnp.zeros_like(q)))
    acc_ref[...] = jnp.zeros(acc_ref.shape, F32)
    m_ref[...] = jnp.full(m_ref.shape, NEG_INF, F32)

    def produce(j, s_ref, c):
        k = k_ref[pl.ds(pl.multiple_of(j * tile, tile), tile), :]
        s_ref[c] = _dot_nt(k, qh[c])

    def consume(j, s_ref, c, masked):
        st = s_ref[c]
        if masked:
            visible = (lax.broadcasted_iota(jnp.int32, (tile, tile), 0)
                       <= lax.broadcasted_iota(jnp.int32, (tile, tile), 1))
            st = jnp.where(visible, st, NEG_INF)
        m_old = m_ref[c]
        m_new = jnp.maximum(m_old, jnp.max(st, axis=0, keepdims=True))
        pt = jnp.exp2(st - m_new).astype(BF16)
        acc_ref[c] = jnp.exp2(m_old - m_new) * acc_ref[c] + _dot(vt_ref[j], pt)
        m_ref[c] = m_new

    bufs = (sa_ref, sb_ref)

    def steps(j0, n, slot):
        for u in range(n):
            for c in range(2):
                produce(j0 + u + 1, bufs[(slot + u + 1) % 2], c)
                consume(j0 + u, bufs[(slot + u) % 2], c, False)

    for c in range(2):
        produce(0, sa_ref, c)

    def group(g, carry):
        steps(g * FLASH_UNROLL, FLASH_UNROLL, 0)
        return carry

    lax.fori_loop(0, i // FLASH_UNROLL, group, 0)
    for r in range(FLASH_UNROLL):
        @pl.when(i % FLASH_UNROLL == r)
        def _(r=r):
            steps(i - r, r, 0)
            for c in range(2):
                consume(i, bufs[r % 2], c, True)

    a0, a1 = acc_ref[0], acc_ref[1]
    ot = (a0[:HEAD_WIDTH] * (1.0 / a0[HEAD_WIDTH:HEAD_WIDTH + 1])
          - a1[:HEAD_WIDTH] * (lam_ref[...] / a1[HEAD_WIDTH:HEAD_WIDTH + 1]))
    ms = jnp.mean(ot * ot, axis=0, keepdims=True)
    ot = ot * lax.rsqrt(ms + NORM_EPS) * sub_ref[...] * (1.0 - LAMBDA_INIT)
    o_ref[...] = ot.T.astype(BF16)


def _lane_groups(x):
    return [x[:, g * HEAD_WIDTH:(g + 1) * HEAD_WIDTH] for g in range(x.shape[1] // HEAD_WIDTH)]


def _paged_body(lam_ref, qm_ref, knew_ref, vnew_ref, sub_ref, bias_ref, nbias_ref, kp_refs, vp_refs, o_ref, s_ref,
                *, nq, nh):
    npages = len(kp_refs)
    qm = qm_ref[...]
    flat = lambda ref: ref[...].reshape(-1, HEAD_WIDTH).astype(BF16)

    kn = flat(knew_ref)
    s_new = _dot_nt(qm, kn) + nbias_ref[...]
    top = None
    bias = bias_ref[...]
    for p in range(npages):
        groups = [g + bias for g in _lane_groups(_dot_nt(qm, flat(kp_refs[p])))]
        s_ref[p] = jnp.concatenate(groups, axis=1)
        page_top = functools.reduce(jnp.maximum, groups)
        top = page_top if top is None else jnp.maximum(top, page_top)
    m = jnp.maximum(jnp.max(top, axis=1, keepdims=True), jnp.max(s_new, axis=1, keepdims=True))
    m_lanes = jnp.broadcast_to(m, (m.shape[0], HEAD_WIDTH))

    p_new = jnp.exp2(s_new - m)
    acc = _dot(p_new.astype(BF16), flat(vnew_ref))
    lsum = None
    for p in range(npages):
        groups = [jnp.exp2(g - m_lanes) for g in _lane_groups(s_ref[p])]
        part = functools.reduce(jnp.add, groups)
        lsum = part if lsum is None else lsum + part
        acc = acc + _dot(jnp.concatenate(groups, axis=1).astype(BF16), flat(vp_refs[p]))
    l = jnp.sum(lsum, axis=1, keepdims=True) + jnp.sum(p_new, axis=1, keepdims=True)
    n = (acc / l).reshape(nh, 2 * nq, HEAD_WIDTH)
    o = n[:, 0:nq, :] - lam_ref[...] * n[:, nq:2 * nq, :]
    o_ref[...] = _finish_heads(o, sub_ref[...]).astype(BF16)


def _attention_kernel(pt_ref, lam_ref, q_ref, k_ref, vt_ref, subc_ref, qm_ref, knew_ref, vnew_ref, subr_ref, bias_ref,
                      nbias_ref, *refs, tile, npages, nq, nh, nseq):
    kp_refs, vp_refs = refs[:npages], refs[npages:2 * npages]
    o_ref, po_ref, acc_ref, sa_ref, sb_ref, m_ref, ps_ref = refs[2 * npages:]
    step = (pl.program_id(0) * pl.num_programs(1) + pl.program_id(1)) * pl.num_programs(2) + pl.program_id(2)

    @pl.when(step < nseq)
    def _():
        _paged_body(lam_ref, qm_ref, knew_ref, vnew_ref, subr_ref, bias_ref, nbias_ref, kp_refs, vp_refs, po_ref,
                    ps_ref, nq=nq, nh=nh)

    per_step = q_ref.shape[0] // tile
    for u in range(per_step):
        rows = pl.ds(u * tile, tile)
        _flash_body(pl.program_id(2) * per_step + u, lam_ref, q_ref.at[rows], k_ref, vt_ref, subc_ref, o_ref.at[rows],
                    acc_ref, sa_ref, sb_ref, m_ref, tile=tile)


def _attention(page_table, lam, subln, q, k, vt, qm, k_new, v_new, cache_k, cache_v):
    b, t, d = q.shape
    nblk, vrows, tile = vt.shape[2:]
    bs, npages = page_table.shape
    _, nq, nh, _ = k_new.shape
    page_keys = cache_k.shape[1]
    nrow = qm.shape[1]
    page_bytes = page_keys * nh * HEAD_WIDTH * 4
    assert 2 * 2 * npages * page_bytes <= PAGED_VMEM_BUDGET, "all pages of a sequence must fit one grid step"
    per_step = FLASH_TILES_PER_STEP if nblk % FLASH_TILES_PER_STEP == 0 else 1
    nsteps = nblk // per_step
    assert bs <= b * nh * nsteps, "one paged sequence per grid step"

    row_head = jnp.arange(nrow)[:, None] // (2 * nq)
    row_query = jnp.arange(nrow)[:, None] % nq
    assert HEAD_WIDTH % nh == 0 and (page_keys * nh) % HEAD_WIDTH == 0
    col = jnp.arange(HEAD_WIDTH)[None, :]
    bias = jnp.where(col % nh == row_head, 0.0, NEG_INF).astype(F32)
    ncol = jnp.arange(nq * nh)[None, :]
    nbias = jnp.where((ncol % nh == row_head) & (ncol // nh <= row_query), 0.0, NEG_INF).astype(F32)

    seq = lambda bi, h, i: jnp.minimum((bi * nh + h) * nsteps + i, bs - 1)
    page_block = (None,) + cache_k.shape[1:]
    page_spec = lambda p: pl.BlockSpec(page_block, lambda bi, h, i, pt: (pt[seq(bi, h, i), p], 0, 0, 0))
    seq4 = lambda shape: pl.BlockSpec((None,) + shape[1:], lambda bi, h, i, pt: (seq(bi, h, i), 0, 0, 0))
    const2 = lambda shape: pl.BlockSpec(shape, lambda bi, h, i, pt: (0, 0))
    grid_spec = pltpu.PrefetchScalarGridSpec(
        num_scalar_prefetch=1,
        grid=(b, nh, nsteps),
        in_specs=[const2((1, 1)),
                  pl.BlockSpec((None, per_step * tile, HEAD_WIDTH), lambda bi, h, i, pt: (bi, i, h)),
                  pl.BlockSpec((None, t, HEAD_WIDTH), lambda bi, h, i, pt: (bi, 0, h)),
                  pl.BlockSpec((None, None, nblk, vrows, tile), lambda bi, h, i, pt: (bi, h, 0, 0, 0)),
                  const2((HEAD_WIDTH, 1)),
                  pl.BlockSpec((None,) + qm.shape[1:], lambda bi, h, i, pt: (seq(bi, h, i), 0, 0)),
                  seq4(k_new.shape), seq4(v_new.shape), const2((1, HEAD_WIDTH)), const2(bias.shape),
                  const2(nbias.shape)]
                 + [page_spec(p) for p in range(npages)] * 2,
        out_specs=[pl.BlockSpec((None, per_step * tile, HEAD_WIDTH), lambda bi, h, i, pt: (bi, i, h)),
                   pl.BlockSpec((None, nh, nq, HEAD_WIDTH), lambda bi, h, i, pt: (seq(bi, h, i), 0, 0, 0))],
        scratch_shapes=[pltpu.VMEM((2, vrows, tile), F32), pltpu.VMEM((2, tile, tile), F32),
                        pltpu.VMEM((2, tile, tile), F32), pltpu.VMEM((2, 1, tile), F32),
                        pltpu.VMEM((npages, nrow, page_keys * nh), F32)],
    )
    return pl.pallas_call(
        functools.partial(_attention_kernel, tile=tile, npages=npages, nq=nq, nh=nh, nseq=bs),
        grid_spec=grid_spec,
        out_shape=[jax.ShapeDtypeStruct((b, t, d), BF16), jax.ShapeDtypeStruct((bs, nh, nq, HEAD_WIDTH), BF16)],
        compiler_params=_params(("arbitrary", "arbitrary", "arbitrary")),
        name="diff_attention",
    )(page_table, lam, q, k, vt, subln.reshape(-1, 1), qm, k_new, v_new, subln, bias, nbias,
      *([cache_k] * npages), *([cache_v] * npages))


def _lam_kernel(q1_ref, k1_ref, q2_ref, k2_ref, o_ref):
    dot1 = jnp.sum(q1_ref[...] * k1_ref[...], axis=-1, keepdims=True)
    dot2 = jnp.sum(q2_ref[...] * k2_ref[...], axis=-1, keepdims=True)
    o_ref[...] = jnp.exp(dot1) - jnp.exp(dot2) + LAMBDA_INIT


def _rope_tables(pos):
    half = HEAD_DIM // 2
    inv = ROPE_THETA ** (-jnp.arange(half, dtype=F32) / half)
    ang = pos.astype(F32)[:, None] * inv[None, :]
    cos, sin = jnp.cos(ang), jnp.sin(ang)
    return jnp.tile(cos, (1, 4)), jnp.tile(jnp.concatenate([-sin, sin], axis=1), (1, 2))


def _mixer(i, x, stride, pos0, st, p):
    groups, r, d = x.shape
    g = p['mix_norm'][i]
    if i == 0:
        x, pool = _pool_mixer(x, st['pool'], g, p['pool_w'], p['pool_scale'], stride, pos0)
        return x, dict(pool=pool), None
    if i == 1:
        x, rgc, rgh = _rglru_mixer(
            x, st['rgc'], st['rgh'], g, p['rg_w_gate'], p['rg_w_in'], p['rg_conv_w'], p['rg_conv_b'], p['rg_wa'],
            p['rg_ba'], p['rg_wx'], p['rg_bx'], p['rg_lambda'], p['rg_w_out'], stride)
        return x, dict(rgc=rgc, rgh=rgh), None
    if i == 2:
        x, cv = _conv_mixer(x, st['cv'], g, p['cv_w_pw1'], p['cv_b_pw1'], p['cv_dw_w'], p['cv_dw_b'],
                            p['cv_ln_g'], p['cv_ln_b'], p['cv_w_pw2'], p['cv_b_pw2'], stride)
        return x, dict(cv=cv), None
    pos = pos0 + jnp.arange(r, dtype=jnp.int32) // stride
    cos, sin = _rope_tables(pos)
    q, k, v, *flash_kv = _qkv(x.reshape(groups * r, d), g, p['at_w_qkv'], p['at_q_norm'], p['at_k_norm'], cos, sin,
                              p['seg'], for_flash=stride == 1)
    return x, dict(k=k, v=v), (q, k, v, *flash_kv)


def _trunk(groups, attend, p):
    xs = [grp['x'] for grp in groups]
    outs = [dict() for _ in groups]
    nlayers = p['mix_norm'].shape[0]

    def ffn_weights(which, i):
        return (i,) + tuple(p[f'{which}_{name}'] for name in ('norm', 'w_gate', 'w_up', 'w_down'))

    def run_ffn(xs, ffns, proj=None):
        ys = _ffn([x.reshape(-1, x.shape[-1]) for x in xs], ffns, proj=proj)
        return [y.reshape(x.shape) for x, y in zip(xs, ys)]

    xs = run_ffn(xs, [ffn_weights('ffn1', 0)])
    for i in range(nlayers):
        qkvs = []
        for n, grp in enumerate(groups):
            xs[n], new_state, qkv = _mixer(i, xs[n], grp['stride'], grp['pos0'], grp['state'], p)
            outs[n].update(new_state)
            qkvs.append(qkv)
        proj = (attend(qkvs), p['at_w_o']) if qkvs[0] is not None else None
        ffns = [ffn_weights('ffn2', i)] + ([ffn_weights('ffn1', i + 1)] if i + 1 < nlayers else [])
        xs = run_ffn(xs, ffns, proj=proj)
    return xs, outs


def kernel(x_prompt, x_sample, state_pool, state_rglru_conv, state_rglru_h, state_conv, cache_k, cache_v, page_table, ffn1_norm, ffn1_w_gate, ffn1_w_up, ffn1_w_down, mix_norm, ffn2_norm, ffn2_w_gate, ffn2_w_up, ffn2_w_down, pool_w, pool_scale, rg_w_gate, rg_w_in, rg_conv_w, rg_conv_b, rg_wa, rg_ba, rg_wx, rg_bx, rg_lambda, rg_w_out, cv_w_pw1, cv_b_pw1, cv_dw_w, cv_dw_b, cv_ln_g, cv_ln_b, cv_w_pw2, cv_b_pw2, at_w_qkv, at_q_norm, at_k_norm, at_lam_q1, at_lam_k1, at_lam_q2, at_lam_k2, at_subln, at_w_o):
    bp, tp, d = x_prompt.shape
    bs, ts, _ = x_sample.shape
    nh = d // HEAD_WIDTH
    past = page_table.shape[1] * cache_k.shape[1]
    bf = lambda w: w.astype(BF16)
    row = lambda v: v.reshape(1, -1).astype(F32)
    rows = lambda v: v.reshape(v.shape[0], 1, -1).astype(F32)

    seg_id = jnp.arange(HEAD_WIDTH) // HEAD_DIM
    p = dict(
        ffn1_norm=rows(ffn1_norm), ffn1_w_gate=bf(ffn1_w_gate), ffn1_w_up=bf(ffn1_w_up), ffn1_w_down=bf(ffn1_w_down),
        mix_norm=rows(mix_norm),
        ffn2_norm=rows(ffn2_norm), ffn2_w_gate=bf(ffn2_w_gate), ffn2_w_up=bf(ffn2_w_up), ffn2_w_down=bf(ffn2_w_down),
        pool_w=bf(pool_w), pool_scale=row(pool_scale),
        rg_w_gate=bf(rg_w_gate), rg_w_in=bf(rg_w_in), rg_conv_w=rg_conv_w, rg_conv_b=row(rg_conv_b),
        rg_wa=bf(rg_wa), rg_ba=row(rg_ba), rg_wx=bf(rg_wx), rg_bx=row(rg_bx), rg_lambda=row(rg_lambda),
        rg_w_out=bf(rg_w_out),
        cv_w_pw1=bf(cv_w_pw1), cv_b_pw1=row(cv_b_pw1), cv_dw_w=cv_dw_w, cv_dw_b=row(cv_dw_b), cv_ln_g=row(cv_ln_g),
        cv_ln_b=row(cv_ln_b), cv_w_pw2=bf(cv_w_pw2), cv_b_pw2=row(cv_b_pw2),
        at_w_qkv=bf(at_w_qkv), at_q_norm=row(jnp.tile(at_q_norm, 2)), at_k_norm=row(jnp.tile(at_k_norm, 2)),
        at_w_o=bf(at_w_o),
        seg=((seg_id[:, None] == seg_id[None, :]).astype(F32) / HEAD_DIM).astype(BF16),
    )
    subln = row(at_subln)
    lam = pl.pallas_call(
        _lam_kernel,
        in_specs=[_const_spec((1, HEAD_DIM))] * 4,
        out_specs=_const_spec((1, 1)),
        out_shape=jax.ShapeDtypeStruct((1, 1), F32),
        name="diff_lambda",
    )(row(at_lam_q1), row(at_lam_k1), row(at_lam_q2), row(at_lam_k2))

    zeros = lambda n: jnp.zeros((bp, n, d), F32)
    prompt = dict(x=x_prompt, stride=1, pos0=0,
                  state=dict(pool=zeros(POOL_BUF), rgc=zeros(RG_CONV - 1), rgh=zeros(1), cv=zeros(CV_WIDTH - 1)))

    def to_tm(a):
        return jnp.swapaxes(a, 0, 1).reshape(1, a.shape[1] * bs, d)

    def from_tm(a):
        return jnp.swapaxes(a.reshape(-1, bs, d), 0, 1)

    def attend(qkvs):
        (qp, _, _, kbp, vtp), (qs, ks, vs) = qkvs
        q5 = from_tm(qs).reshape(bs, ts, nh, 2, HEAD_DIM)
        eye = jnp.eye(2, dtype=qs.dtype)
        qm = jnp.einsum('bqhcd,ce->bhcqed', q5, eye).reshape(bs, nh * 2 * ts, HEAD_WIDTH)
        k_new = from_tm(ks).reshape(bs, ts, nh, HEAD_WIDTH)
        v_new = from_tm(vs).reshape(bs, ts, nh, HEAD_WIDTH)
        shape = (bp, tp, d)
        heads_p, heads_s = _attention(page_table, lam, subln, qp.reshape(shape), kbp.reshape(shape), vtp, qm, k_new,
                                      v_new, cache_k, cache_v)
        return [heads_p.reshape(-1, d), jnp.transpose(heads_s, (2, 0, 1, 3)).reshape(ts * bs, d)]

    sample = dict(x=to_tm(x_sample), stride=bs, pos0=past,
                  state=dict(pool=to_tm(state_pool), rgc=to_tm(state_rglru_conv),
                             rgh=state_rglru_h.reshape(1, bs, d), cv=to_tm(state_conv)))
    (y_p, y_s), (o_p, o_s) = _trunk([prompt, sample], attend, p)

    kv4 = lambda a, b, t: a.reshape(b, t, nh, HEAD_WIDTH)
    return (y_p, from_tm(y_s),
            o_p['pool'], from_tm(o_s['pool']),
            o_p['rgc'], from_tm(o_s['rgc']),
            o_p['rgh'].reshape(bp, d), o_s['rgh'].reshape(bs, d),
            o_p['cv'], from_tm(o_s['cv']),
            kv4(o_p['k'], bp, tp), kv4(o_p['v'], bp, tp),
            kv4(from_tm(o_s['k']), bs, ts), kv4(from_tm(o_s['v']), bs, ts))
```

```python
import functools
import math

import jax
import jax.numpy as jnp
from jax import lax
from jax.experimental import pallas as pl
from jax.experimental.pallas import tpu as pltpu

F32 = jnp.float32
BF16 = jnp.bfloat16

NORM_EPS = 1e-6
POOL_WINDOWS = (2, 4, 8, 16)
POOL_BUF = max(POOL_WINDOWS) - 1
RG_CONV = 4
RG_C = 8.0
CV_WIDTH = 31
HEAD_DIM = 64
HEAD_WIDTH = 2 * HEAD_DIM
ROPE_THETA = 10000.0
ATT_LAYER = 3
LAMBDA_INIT = 0.8 - 0.6 * math.exp(-0.3 * ATT_LAYER)
NEG_INF = -1e30

Q_SCALE = HEAD_DIM ** -0.5 * math.log2(math.e)

SUBLANES = 8
VMEM_LIMIT = 56 * 1024 * 1024
PAGED_VMEM_BUDGET = 36 * 1024 * 1024
SEQ_TILE = 512
FFN_CHUNK = 256
FLASH_TILES_PER_STEP = 2
FLASH_UNROLL = 4
VT_ROWS = HEAD_WIDTH + 16


def _params(sem):
    return pltpu.CompilerParams(dimension_semantics=sem, vmem_limit_bytes=VMEM_LIMIT)


def _rms(x, g):
    ms = jnp.mean(x * x, axis=-1, keepdims=True)
    return x * lax.rsqrt(ms + NORM_EPS) * g


def _dot(a, b):
    return jnp.dot(a, b, preferred_element_type=F32)


def _dot_nt(a, b):
    return lax.dot_general(a, b, (((1,), (1,)), ((), ())), preferred_element_type=F32)


def _const_spec(shape):
    n = len(shape)
    return pl.BlockSpec(shape, lambda *_: (0,) * n)


def _ffn_kernel(*refs, nstreams, tiles, with_proj, nffn):
    per = 2 if with_proj else 1
    ins = refs[:nstreams * per]
    rest = refs[nstreams * per:]
    if with_proj:
        wo_ref, rest = rest[0], rest[1:]
    ffns = [rest[4 * n:4 * n + 4] for n in range(nffn)]
    outs = rest[4 * nffn:4 * nffn + nstreams]
    h_ref = rest[4 * nffn + nstreams]
    i = pl.program_id(0)
    start = 0
    for s in range(nstreams):
        @pl.when((i >= start) & (i < start + tiles[s]))
        def _(s=s):
            x = ins[s * per][...]
            if with_proj:
                x = x + _dot(ins[s * per + 1][...], wo_ref[...])
            rows = x.shape[0]
            for g_ref, wg_ref, wu_ref, wd_ref in ffns:
                u = _rms(x, g_ref[...]).astype(BF16)
                for c in range(wg_ref.shape[1] // FFN_CHUNK):
                    sl = slice(c * FFN_CHUNK, (c + 1) * FFN_CHUNK)
                    gate = _dot(u, wg_ref[:, sl])
                    up = _dot(u, wu_ref[:, sl])
                    h_ref[0:rows, sl] = (gate * jax.nn.sigmoid(gate) * up).astype(BF16)
                x = x + 0.5 * _dot(h_ref[0:rows, :], wd_ref[...])
            outs[s][...] = x
        start += tiles[s]


def _ffn(xs, ffns, proj=None):
    d = xs[0].shape[1]
    d_ff = ffns[0][2].shape[2]
    tms = [min(SEQ_TILE, x.shape[0]) for x in xs]
    tiles = [x.shape[0] // tm for x, tm in zip(xs, tms)]
    offsets = [sum(tiles[:s]) for s in range(len(xs))]
    single = pl.Buffered(1)

    def stream_spec(s, width):
        return pl.BlockSpec((tms[s], width),
                            lambda i, s=s: (jnp.clip(i - offsets[s], 0, tiles[s] - 1), 0))

    args, specs = [], []
    for s, x in enumerate(xs):
        args.append(x)
        specs.append(stream_spec(s, d))
        if proj is not None:
            args.append(proj[0][s])
            specs.append(stream_spec(s, proj[0][s].shape[1]))
    if proj is not None:
        args.append(proj[1])
        specs.append(pl.BlockSpec(proj[1].shape, lambda i: (0, 0), pipeline_mode=single))
    for layer, g, wg, wu, wd in ffns:
        pick = lambda i, layer=layer: (layer, 0, 0)
        args += [g, wg, wu, wd]
        specs += [pl.BlockSpec((None, 1, d), pick),
                  pl.BlockSpec((None, d, d_ff), pick, pipeline_mode=single),
                  pl.BlockSpec((None, d, d_ff), pick, pipeline_mode=single),
                  pl.BlockSpec((None, d_ff, d), pick, pipeline_mode=single)]
    return pl.pallas_call(
        functools.partial(_ffn_kernel, nstreams=len(xs), tiles=tuple(tiles), with_proj=proj is not None,
                          nffn=len(ffns)),
        grid=(sum(tiles),),
        in_specs=specs,
        out_specs=[stream_spec(s, d) for s in range(len(xs))],
        out_shape=[jax.ShapeDtypeStruct(x.shape, F32) for x in xs],
        scratch_shapes=[pltpu.VMEM((max(tms), d_ff), BF16)],
        compiler_params=_params(("arbitrary",)),
        name="ffn_proj" if proj is not None else "ffn",
    )(*args)


def _load_history(ext_ref, prev_ref, halo_rows):
    pb = prev_ref.shape[0]
    if halo_rows > pb:
        ext_ref[0:halo_rows - pb, :] = jnp.zeros((halo_rows - pb, ext_ref.shape[1]), F32)
    ext_ref[halo_rows - pb:halo_rows, :] = prev_ref[...]


def _pool_kernel(x_ref, prev_ref, g_ref, w_ref, sc_ref, o_ref, st_ref, ext_ref, *, stride, tt, halo, pos0, nsteps):
    t = pl.program_id(1)
    rows, hr = tt * stride, halo * stride
    pb = prev_ref.shape[0]

    @pl.when(t == 0)
    def _():
        _load_history(ext_ref, prev_ref, hr)

    x = x_ref[...]
    u = _rms(x, g_ref[...])
    ext_ref[hr:hr + rows, :] = u
    step = lax.broadcasted_iota(jnp.int32, (rows, 1), 0) >> int(math.log2(stride))
    pos = pos0 + t * tt + step
    gw = x.shape[1] // len(POOL_WINDOWS)
    for gi, win in enumerate(POOL_WINDOWS):
        cs = slice(gi * gw, (gi + 1) * gw)
        s = u[:, cs]
        for k in range(1, win):
            s = s + ext_ref[hr - k * stride:hr - k * stride + rows, cs]
        cnt = jnp.minimum(pos + 1, win).astype(F32)
        diff = (s / cnt - u[:, cs]).astype(BF16)
        o_ref[:, cs] = x[:, cs] + _dot(diff, w_ref[gi]) * sc_ref[:, cs]

    @pl.when(t == nsteps - 1)
    def _():
        st_ref[...] = ext_ref[hr + rows - pb:hr + rows, :]

    if nsteps > 1:
        ext_ref[0:hr, :] = ext_ref[rows:rows + hr, :]


def _seq_specs(rows, d):
    return pl.BlockSpec((None, rows, d), lambda g, t: (g, t, 0))


def _state_spec(shape):
    return pl.BlockSpec((None,) + tuple(shape[1:]), lambda g, t: (g,) + (0,) * (len(shape) - 1))


def _tiling(x, stride):
    groups, r, d = x.shape
    tt = min(SEQ_TILE, r) if stride == 1 else r // stride
    rows = tt * stride
    return groups, r, d, tt, rows, r // rows


def _pool_mixer(x, prev, g, w, scale, stride, pos0):
    groups, r, d, tt, rows, nsteps = _tiling(x, stride)
    halo = 16 if stride == 1 else POOL_BUF
    return pl.pallas_call(
        functools.partial(_pool_kernel, stride=stride, tt=tt, halo=halo, pos0=pos0, nsteps=nsteps),
        grid=(groups, nsteps),
        in_specs=[_seq_specs(rows, d), _state_spec(prev.shape), _const_spec(g.shape), _const_spec(w.shape),
                  _const_spec(scale.shape)],
        out_specs=[_seq_specs(rows, d), _state_spec(prev.shape)],
        out_shape=[jax.ShapeDtypeStruct(x.shape, F32), jax.ShapeDtypeStruct(prev.shape, F32)],
        scratch_shapes=[pltpu.VMEM((halo * stride + rows, d), F32)],
        compiler_params=_params(("arbitrary", "arbitrary")),
        name="pool_mixer",
    )(x, prev, g, w, scale)


def _rglru_kernel(x_ref, cprev_ref, hprev_ref, g_ref, wgate_ref, win_ref, cw_ref, cb_ref, wa_ref, ba_ref, wx_ref,
                  bx_ref, lam_ref, wout_ref, o_ref, cst_ref, hst_ref, ext_ref, h_ref, a_ref, b_ref,
                  *, stride, tt, halo, nsteps):
    t = pl.program_id(1)
    rows, hr = tt * stride, halo * stride
    pb = cprev_ref.shape[0]

    @pl.when(t == 0)
    def _():
        _load_history(ext_ref, cprev_ref, hr)
        h_ref[...] = hprev_ref[...]

    x = x_ref[...]
    u = _rms(x, g_ref[...]).astype(BF16)
    gate = jax.nn.gelu(_dot(u, wgate_ref[...]))
    zin = _dot(u, win_ref[...])
    ext_ref[hr:hr + rows, :] = zin
    z = cb_ref[...] + cw_ref[RG_CONV - 1:RG_CONV, :] * zin
    for k in range(RG_CONV - 1):
        back = (RG_CONV - 1 - k) * stride
        z = z + cw_ref[k:k + 1, :] * ext_ref[hr - back:hr - back + rows, :]
    zb = z.astype(BF16)
    lam = lam_ref[...]
    softplus_neg_lam = jnp.maximum(-lam, 0.0) + jnp.log1p(jnp.exp(-jnp.abs(lam)))
    nblk = wa_ref.shape[0]
    bw = x.shape[1] // nblk
    for n in range(nblk):
        cs = slice(n * bw, (n + 1) * bw)
        r = jax.nn.sigmoid(_dot(zb[:, cs], wa_ref[n]) + ba_ref[:, cs])
        i = jax.nn.sigmoid(_dot(zb[:, cs], wx_ref[n]) + bx_ref[:, cs])
        log_a = -RG_C * r * softplus_neg_lam[:, cs]
        a = jnp.exp(log_a)
        a_ref[:, cs] = a
        b_ref[:, cs] = jnp.sqrt(1.0 - a * a) * (i * z[:, cs])

    if stride == 1:
        def chunk(c, h):
            r0 = pl.multiple_of(c * SUBLANES, SUBLANES)
            a = a_ref[pl.ds(r0, SUBLANES), :]
            b = b_ref[pl.ds(r0, SUBLANES), :]
            row = lax.broadcasted_iota(jnp.int32, a.shape, 0)
            for s in (1, 2, 4):
                keep = row >= s
                b = jnp.where(keep, a * pltpu.roll(b, s, axis=0) + b, b)
                a = jnp.where(keep, a * pltpu.roll(a, s, axis=0), a)
            hs = a * h + b
            b_ref[pl.ds(r0, SUBLANES), :] = hs
            return hs[SUBLANES - 1:SUBLANES, :]

        h_ref[...] = lax.fori_loop(0, rows // SUBLANES, chunk, h_ref[...], unroll=4)
    else:
        h = h_ref[...]
        for s in range(tt):
            sl = slice(s * stride, (s + 1) * stride)
            h = a_ref[sl, :] * h + b_ref[sl, :]
            b_ref[sl, :] = h
        h_ref[...] = h

    y = _dot((gate * b_ref[...]).astype(BF16), wout_ref[...])
    o_ref[...] = x + y

    @pl.when(t == nsteps - 1)
    def _():
        cst_ref[...] = ext_ref[hr + rows - pb:hr + rows, :]
        hst_ref[...] = h_ref[...]

    if nsteps > 1:
        ext_ref[0:hr, :] = ext_ref[rows:rows + hr, :]


def _rglru_mixer(x, cprev, hprev, g, wgate, win, cw, cb, wa, ba, wx, bx, lam, wout, stride):
    groups, r, d, tt, rows, nsteps = _tiling(x, stride)
    halo = SUBLANES if stride == 1 else RG_CONV - 1
    consts = [g, wgate, win, cw, cb, wa, ba, wx, bx, lam, wout]
    return pl.pallas_call(
        functools.partial(_rglru_kernel, stride=stride, tt=tt, halo=halo, nsteps=nsteps),
        grid=(groups, nsteps),
        in_specs=[_seq_specs(rows, d), _state_spec(cprev.shape), _state_spec(hprev.shape)]
                 + [_const_spec(c.shape) for c in consts],
        out_specs=[_seq_specs(rows, d), _state_spec(cprev.shape), _state_spec(hprev.shape)],
        out_shape=[jax.ShapeDtypeStruct(x.shape, F32), jax.ShapeDtypeStruct(cprev.shape, F32),
                   jax.ShapeDtypeStruct(hprev.shape, F32)],
        scratch_shapes=[pltpu.VMEM((halo * stride + rows, d), F32), pltpu.VMEM((stride, d), F32),
                        pltpu.VMEM((rows, d), F32), pltpu.VMEM((rows, d), F32)],
        compiler_params=_params(("arbitrary", "arbitrary")),
        name="rglru_mixer",
    )(x, cprev, hprev, *consts)


def _convmod_kernel(x_ref, prev_ref, g_ref, w1_ref, b1_ref, dw_ref, dwb_ref, lng_ref, lnb_ref, w2_ref, b2_ref,
                    o_ref, st_ref, ext_ref, *, stride, tt, halo, nsteps):
    t = pl.program_id(1)
    rows, hr = tt * stride, halo * stride
    pb = prev_ref.shape[0]
    d = x_ref.shape[1]

    @pl.when(t == 0)
    def _():
        _load_history(ext_ref, prev_ref, hr)

    x = x_ref[...]
    u = _rms(x, g_ref[...]).astype(BF16)
    hc = _dot(u, w1_ref[...]) + b1_ref[...]
    glu = hc[:, :d] * jax.nn.sigmoid(hc[:, d:])
    ext_ref[hr:hr + rows, :] = glu
    c = dwb_ref[...]
    if stride == 1:
        for b in range(SUBLANES):
            phase = None
            for a in range((CV_WIDTH - 1 - b) // SUBLANES + 1):
                k = CV_WIDTH - 1 - (SUBLANES * a + b)
                start = hr - SUBLANES * (a + 1)
                term = dw_ref[k:k + 1, :] * ext_ref[start:start + rows + SUBLANES, :]
                phase = term if phase is None else phase + term
            c = c + phase[SUBLANES - b:SUBLANES - b + rows, :]
    else:
        for k in range(CV_WIDTH):
            back = (CV_WIDTH - 1 - k) * stride
            c = c + dw_ref[k:k + 1, :] * ext_ref[hr - back:hr - back + rows, :]
    cc = c - jnp.mean(c, axis=-1, keepdims=True)
    y = cc * lax.rsqrt(jnp.mean(cc * cc, axis=-1, keepdims=True) + NORM_EPS)
    y = y * lng_ref[...] + lnb_ref[...]
    y = (y * jax.nn.sigmoid(y)).astype(BF16)
    o_ref[...] = x + _dot(y, w2_ref[...]) + b2_ref[...]

    @pl.when(t == nsteps - 1)
    def _():
        st_ref[...] = ext_ref[hr + rows - pb:hr + rows, :]

    if nsteps > 1:
        ext_ref[0:hr, :] = ext_ref[rows:rows + hr, :]


def _conv_mixer(x, prev, g, w1, b1, dw, dwb, lng, lnb, w2, b2, stride):
    groups, r, d, tt, rows, nsteps = _tiling(x, stride)
    halo = 32 if stride == 1 else CV_WIDTH - 1
    consts = [g, w1, b1, dw, dwb, lng, lnb, w2, b2]
    return pl.pallas_call(
        functools.partial(_convmod_kernel, stride=stride, tt=tt, halo=halo, nsteps=nsteps),
        grid=(groups, nsteps),
        in_specs=[_seq_specs(rows, d), _state_spec(prev.shape)] + [_const_spec(c.shape) for c in consts],
        out_specs=[_seq_specs(rows, d), _state_spec(prev.shape)],
        out_shape=[jax.ShapeDtypeStruct(x.shape, F32), jax.ShapeDtypeStruct(prev.shape, F32)],
        scratch_shapes=[pltpu.VMEM((halo * stride + rows, d), F32)],
        compiler_params=_params(("arbitrary", "arbitrary")),
        name="conv_mixer",
    )(x, prev, g, w1, b1, dw, dwb, lng, lnb, w2, b2)


def _qkv_kernel(x_ref, g_ref, w_ref, qg_ref, kg_ref, cos_ref, sin_ref, seg_ref, q_ref, k_ref, v_ref, *flash_refs):
    d = x_ref.shape[1]
    u = _rms(x_ref[...], g_ref[...]).astype(BF16)
    qkv = _dot(u, w_ref[...])
    cos, sin = cos_ref[...], sin_ref[...]
    seg = seg_ref[...]
    lane = lax.broadcasted_iota(jnp.int32, (1, HEAD_WIDTH), 1)
    first_half = (lane & (HEAD_DIM - 1)) < HEAD_DIM // 2

    def norm_rope(blk, gain):
        sq = blk * blk
        hi = sq.astype(BF16)
        lo = (sq - hi.astype(F32)).astype(BF16)
        ms = _dot(hi, seg) + _dot(lo, seg)
        y = blk * lax.rsqrt(ms + NORM_EPS) * gain
        partner = jnp.where(first_half, pltpu.roll(y, HEAD_WIDTH - HEAD_DIM // 2, axis=1),
                            pltpu.roll(y, HEAD_DIM // 2, axis=1))
        return y * cos + partner * sin

    for h in range(d // HEAD_WIDTH):
        cs = slice(h * HEAD_WIDTH, (h + 1) * HEAD_WIDTH)
        q = norm_rope(qkv[:, cs], qg_ref[...])
        q_ref[:, cs] = (q * Q_SCALE).astype(BF16)
        k = norm_rope(qkv[:, d + h * HEAD_WIDTH:d + (h + 1) * HEAD_WIDTH], kg_ref[...])
        k_ref[:, cs] = k
        v = qkv[:, 2 * d + h * HEAD_WIDTH:2 * d + (h + 1) * HEAD_WIDTH]
        v_ref[:, cs] = v
        if flash_refs:
            kb_ref, vt_ref = flash_refs
            kb_ref[:, cs] = k.astype(BF16)
            vt_ref[h, 0:HEAD_WIDTH, :] = v.T.astype(BF16)
            pad = lax.broadcasted_iota(jnp.int32, (VT_ROWS - HEAD_WIDTH, v.shape[0]), 0)
            vt_ref[h, HEAD_WIDTH:VT_ROWS, :] = (pad == 0).astype(BF16)


def _qkv(x, g, w, qg, kg, cos, sin, seg, for_flash):
    n, d = x.shape
    tm = min(SEQ_TILE, n)
    nblk = cos.shape[0] // tm
    nh = d // HEAD_WIDTH
    row = pl.BlockSpec((tm, d), lambda i: (i, 0))
    tab = pl.BlockSpec((tm, HEAD_WIDTH), lambda i: (i % nblk, 0))
    out_specs = [row] * 3
    out_shape = [jax.ShapeDtypeStruct((n, d), dt) for dt in (BF16, F32, F32)]
    if for_flash:
        out_specs += [row, pl.BlockSpec((None, nh, None, VT_ROWS, tm), lambda i: (i // nblk, 0, i % nblk, 0, 0))]
        out_shape += [jax.ShapeDtypeStruct((n, d), BF16),
                      jax.ShapeDtypeStruct((n // (nblk * tm), nh, nblk, VT_ROWS, tm), BF16)]
    return pl.pallas_call(
        _qkv_kernel,
        grid=(n // tm,),
        in_specs=[row, _const_spec(g.shape), _const_spec(w.shape), _const_spec(qg.shape), _const_spec(kg.shape),
                  tab, tab, _const_spec(seg.shape)],
        out_specs=out_specs,
        out_shape=out_shape,
        compiler_params=_params(("arbitrary",)),
        name="qkv_rope",
    )(x, g, w, qg, kg, cos, sin, seg)


def _finish_heads(o, subln):
    ms = jnp.mean(o * o, axis=-1, keepdims=True)
    return o * lax.rsqrt(ms + NORM_EPS) * subln * (1.0 - LAMBDA_INIT)


def _flash_body(i, lam_ref, q_ref, k_ref, vt_ref, sub_ref, o_ref, acc_ref, sa_ref, sb_ref, m_ref, *, tile):
    q = q_ref[...]
    lane = lax.broadcasted_iota(jnp.int32, q.shape, 1)
    qh = (jnp.where(lane < HEAD_DIM, q, jnp.zeros_like(q)), jnp.where(lane >= HEAD_DIM, q, jnp.zeros_like(q)))
    acc_ref[...] = jnp.zeros(acc_ref.shape, F32)
    m_ref[...] = jnp.full(m_ref.shape, NEG_INF, F32)

    def produce(j, s_ref, c):
        k = k_ref[pl.ds(pl.multiple_of(j * tile, tile), tile), :]
        s_ref[c] = _dot_nt(k, qh[c])

    def consume(j, s_ref, c, masked):
        st = s_ref[c]
        if masked:
            visible = (lax.broadcasted_iota(jnp.int32, (tile, tile), 0)
                       <= lax.broadcasted_iota(jnp.int32, (tile, tile), 1))
            st = jnp.where(visible, st, NEG_INF)
        m_old = m_ref[c]
        m_new = jnp.maximum(m_old, jnp.max(st, axis=0, keepdims=True))
        pt = jnp.exp2(st - m_new).astype(BF16)
        acc_ref[c] = jnp.exp2(m_old - m_new) * acc_ref[c] + _dot(vt_ref[j], pt)
        m_ref[c] = m_new

    bufs = (sa_ref, sb_ref)

    def steps(j0, n, slot):
        for u in range(n):
            for c in range(2):
                produce(j0 + u + 1, bufs[(slot + u + 1) % 2], c)
                consume(j0 + u, bufs[(slot + u) % 2], c, False)

    for c in range(2):
        produce(0, sa_ref, c)

    def group(g, carry):
        steps(g * FLASH_UNROLL, FLASH_UNROLL, 0)
        return carry

    lax.fori_loop(0, i // FLASH_UNROLL, group, 0)
    for r in range(FLASH_UNROLL):
        @pl.when(i % FLASH_UNROLL == r)
        def _(r=r):
            steps(i - r, r, 0)
            for c in range(2):
                consume(i, bufs[r % 2], c, True)

    a0, a1 = acc_ref[0], acc_ref[1]
    ot = (a0[:HEAD_WIDTH] * (1.0 / a0[HEAD_WIDTH:HEAD_WIDTH + 1])
          - a1[:HEAD_WIDTH] * (lam_ref[...] / a1[HEAD_WIDTH:HEAD_WIDTH + 1]))
    ms = jnp.mean(ot * ot, axis=0, keepdims=True)
    ot = ot * lax.rsqrt(ms + NORM_EPS) * sub_ref[...] * (1.0 - LAMBDA_INIT)
    o_ref[...] = ot.T.astype(BF16)


def _lane_groups(x):
    return [x[:, g * HEAD_WIDTH:(g + 1) * HEAD_WIDTH] for g in range(x.shape[1] // HEAD_WIDTH)]


def _paged_body(lam_ref, qm_ref, knew_ref, vnew_ref, sub_ref, bias_ref, nbias_ref, kp_refs, vp_refs, o_ref, s_ref,
                *, nq, nh):
    npages = len(kp_refs)
    qm = qm_ref[...]
    flat = lambda ref: ref[...].reshape(-1, HEAD_WIDTH).astype(BF16)

    kn = flat(knew_ref)
    s_new = _dot_nt(qm, kn) + nbias_ref[...]
    top = None
    bias = bias_ref[...]
    for p in range(npages):
        groups = [g + bias for g in _lane_groups(_dot_nt(qm, flat(kp_refs[p])))]
        s_ref[p] = jnp.concatenate(groups, axis=1)
        page_top = functools.reduce(jnp.maximum, groups)
        top = page_top if top is None else jnp.maximum(top, page_top)
    m = jnp.maximum(jnp.max(top, axis=1, keepdims=True), jnp.max(s_new, axis=1, keepdims=True))
    m_lanes = jnp.broadcast_to(m, (m.shape[0], HEAD_WIDTH))

    p_new = jnp.exp2(s_new - m)
    acc = _dot(p_new.astype(BF16), flat(vnew_ref))
    lsum = None
    for p in range(npages):
        groups = [jnp.exp2(g - m_lanes) for g in _lane_groups(s_ref[p])]
        part = functools.reduce(jnp.add, groups)
        lsum = part if lsum is None else lsum + part
        acc = acc + _dot(jnp.concatenate(groups, axis=1).astype(BF16), flat(vp_refs[p]))
    l = jnp.sum(lsum, axis=1, keepdims=True) + jnp.sum(p_new, axis=1, keepdims=True)
    n = (acc / l).reshape(nh, 2 * nq, HEAD_WIDTH)
    o = n[:, 0:nq, :] - lam_ref[...] * n[:, nq:2 * nq, :]
    o_ref[...] = _finish_heads(o, sub_ref[...]).astype(BF16)


def _attention_kernel(pt_ref, lam_ref, q_ref, k_ref, vt_ref, subc_ref, qm_ref, knew_ref, vnew_ref, subr_ref, bias_ref,
                      nbias_ref, *refs, tile, npages, nq, nh, nseq):
    kp_refs, vp_refs = refs[:npages], refs[npages:2 * npages]
    o_ref, po_ref, acc_ref, sa_ref, sb_ref, m_ref, ps_ref = refs[2 * npages:]
    step = (pl.program_id(0) * pl.num_programs(1) + pl.program_id(1)) * pl.num_programs(2) + pl.program_id(2)

    @pl.when(step < nseq)
    def _():
        _paged_body(lam_ref, qm_ref, knew_ref, vnew_ref, subr_ref, bias_ref, nbias_ref, kp_refs, vp_refs, po_ref,
                    ps_ref, nq=nq, nh=nh)

    per_step = q_ref.shape[0] // tile

    def query_tile(u, carry):
        rows = pl.ds(pl.multiple_of(u * tile, tile), tile)
        _flash_body(pl.program_id(2) * per_step + u, lam_ref, q_ref.at[rows], k_ref, vt_ref, subc_ref, o_ref.at[rows],
                    acc_ref, sa_ref, sb_ref, m_ref, tile=tile)
        return carry

    lax.fori_loop(0, per_step, query_tile, 0)


def _attention(page_table, lam, subln, q, k, vt, qm, k_new, v_new, cache_k, cache_v):
    b, t, d = q.shape
    nblk, vrows, tile = vt.shape[2:]
    bs, npages = page_table.shape
    _, nq, nh, _ = k_new.shape
    page_keys = cache_k.shape[1]
    nrow = qm.shape[1]
    page_bytes = page_keys * nh * HEAD_WIDTH * 4
    assert 2 * 2 * npages * page_bytes <= PAGED_VMEM_BUDGET, "all pages of a sequence must fit one grid step"
    per_step = FLASH_TILES_PER_STEP if nblk % FLASH_TILES_PER_STEP == 0 else 1
    nsteps = nblk // per_step
    assert bs <= b * nh * nsteps, "one paged sequence per grid step"

    row_head = jnp.arange(nrow)[:, None] // (2 * nq)
    row_query = jnp.arange(nrow)[:, None] % nq
    assert HEAD_WIDTH % nh == 0 and (page_keys * nh) % HEAD_WIDTH == 0
    col = jnp.arange(HEAD_WIDTH)[None, :]
    bias = jnp.where(col % nh == row_head, 0.0, NEG_INF).astype(F32)
    ncol = jnp.arange(nq * nh)[None, :]
    nbias = jnp.where((ncol % nh == row_head) & (ncol // nh <= row_query), 0.0, NEG_INF).astype(F32)

    seq = lambda bi, h, i: jnp.minimum((bi * nh + h) * nsteps + i, bs - 1)
    page_block = (None,) + cache_k.shape[1:]
    page_spec = lambda p: pl.BlockSpec(page_block, lambda bi, h, i, pt: (pt[seq(bi, h, i), p], 0, 0, 0))
    seq4 = lambda shape: pl.BlockSpec((None,) + shape[1:], lambda bi, h, i, pt: (seq(bi, h, i), 0, 0, 0))
    const2 = lambda shape: pl.BlockSpec(shape, lambda bi, h, i, pt: (0, 0))
    grid_spec = pltpu.PrefetchScalarGridSpec(
        num_scalar_prefetch=1,
        grid=(b, nh, nsteps),
        in_specs=[const2((1, 1)),
                  pl.BlockSpec((None, per_step * tile, HEAD_WIDTH), lambda bi, h, i, pt: (bi, i, h)),
                  pl.BlockSpec((None, t, HEAD_WIDTH), lambda bi, h, i, pt: (bi, 0, h)),
                  pl.BlockSpec((None, None, nblk, vrows, tile), lambda bi, h, i, pt: (bi, h, 0, 0, 0)),
                  const2((HEAD_WIDTH, 1)),
                  pl.BlockSpec((None,) + qm.shape[1:], lambda bi, h, i, pt: (seq(bi, h, i), 0, 0)),
                  seq4(k_new.shape), seq4(v_new.shape), const2((1, HEAD_WIDTH)), const2(bias.shape),
                  const2(nbias.shape)]
                 + [page_spec(p) for p in range(npages)] * 2,
        out_specs=[pl.BlockSpec((None, per_step * tile, HEAD_WIDTH), lambda bi, h, i, pt: (bi, i, h)),
                   pl.BlockSpec((None, nh, nq, HEAD_WIDTH), lambda bi, h, i, pt: (seq(bi, h, i), 0, 0, 0))],
        scratch_shapes=[pltpu.VMEM((2, vrows, tile), F32), pltpu.VMEM((2, tile, tile), F32),
                        pltpu.VMEM((2, tile, tile), F32), pltpu.VMEM((2, 1, tile), F32),
                        pltpu.VMEM((npages, nrow, page_keys * nh), F32)],
    )
    return pl.pallas_call(
        functools.partial(_attention_kernel, tile=tile, npages=npages, nq=nq, nh=nh, nseq=bs),
        grid_spec=grid_spec,
        out_shape=[jax.ShapeDtypeStruct((b, t, d), BF16), jax.ShapeDtypeStruct((bs, nh, nq, HEAD_WIDTH), BF16)],
        compiler_params=_params(("arbitrary", "arbitrary", "arbitrary")),
        name="diff_attention",
    )(page_table, lam, q, k, vt, subln.reshape(-1, 1), qm, k_new, v_new, subln, bias, nbias,
      *([cache_k] * npages), *([cache_v] * npages))


def _lam_kernel(q1_ref, k1_ref, q2_ref, k2_ref, o_ref):
    dot1 = jnp.sum(q1_ref[...] * k1_ref[...], axis=-1, keepdims=True)
    dot2 = jnp.sum(q2_ref[...] * k2_ref[...], axis=-1, keepdims=True)
    o_ref[...] = jnp.exp(dot1) - jnp.exp(dot2) + LAMBDA_INIT


def _rope_tables(pos):
    half = HEAD_DIM // 2
    inv = ROPE_THETA ** (-jnp.arange(half, dtype=F32) / half)
    ang = pos.astype(F32)[:, None] * inv[None, :]
    cos, sin = jnp.cos(ang), jnp.sin(ang)
    return jnp.tile(cos, (1, 4)), jnp.tile(jnp.concatenate([-sin, sin], axis=1), (1, 2))


def _mixer(i, x, stride, pos0, st, p):
    groups, r, d = x.shape
    g = p['mix_norm'][i]
    if i == 0:
        x, pool = _pool_mixer(x, st['pool'], g, p['pool_w'], p['pool_scale'], stride, pos0)
        return x, dict(pool=pool), None
    if i == 1:
        x, rgc, rgh = _rglru_mixer(
            x, st['rgc'], st['rgh'], g, p['rg_w_gate'], p['rg_w_in'], p['rg_conv_w'], p['rg_conv_b'], p['rg_wa'],
            p['rg_ba'], p['rg_wx'], p['rg_bx'], p['rg_lambda'], p['rg_w_out'], stride)
        return x, dict(rgc=rgc, rgh=rgh), None
    if i == 2:
        x, cv = _conv_mixer(x, st['cv'], g, p['cv_w_pw1'], p['cv_b_pw1'], p['cv_dw_w'], p['cv_dw_b'],
                            p['cv_ln_g'], p['cv_ln_b'], p['cv_w_pw2'], p['cv_b_pw2'], stride)
        return x, dict(cv=cv), None
    pos = pos0 + jnp.arange(r, dtype=jnp.int32) // stride
    cos, sin = _rope_tables(pos)
    q, k, v, *flash_kv = _qkv(x.reshape(groups * r, d), g, p['at_w_qkv'], p['at_q_norm'], p['at_k_norm'], cos, sin,
                              p['seg'], for_flash=stride == 1)
    return x, dict(k=k, v=v), (q, k, v, *flash_kv)


def _trunk(groups, attend, p):
    xs = [grp['x'] for grp in groups]
    outs = [dict() for _ in groups]
    nlayers = p['mix_norm'].shape[0]

    def ffn_weights(which, i):
        return (i,) + tuple(p[f'{which}_{name}'] for name in ('norm', 'w_gate', 'w_up', 'w_down'))

    def run_ffn(xs, ffns, proj=None):
        ys = _ffn([x.reshape(-1, x.shape[-1]) for x in xs], ffns, proj=proj)
        return [y.reshape(x.shape) for x, y in zip(xs, ys)]

    xs = run_ffn(xs, [ffn_weights('ffn1', 0)])
    for i in range(nlayers):
        qkvs = []
        for n, grp in enumerate(groups):
            xs[n], new_state, qkv = _mixer(i, xs[n], grp['stride'], grp['pos0'], grp['state'], p)
            outs[n].update(new_state)
            qkvs.append(qkv)
        proj = (attend(qkvs), p['at_w_o']) if qkvs[0] is not None else None
        ffns = [ffn_weights('ffn2', i)] + ([ffn_weights('ffn1', i + 1)] if i + 1 < nlayers else [])
        xs = run_ffn(xs, ffns, proj=proj)
    return xs, outs


def kernel(x_prompt, x_sample, state_pool, state_rglru_conv, state_rglru_h, state_conv, cache_k, cache_v, page_table, ffn1_norm, ffn1_w_gate, ffn1_w_up, ffn1_w_down, mix_norm, ffn2_norm, ffn2_w_gate, ffn2_w_up, ffn2_w_down, pool_w, pool_scale, rg_w_gate, rg_w_in, rg_conv_w, rg_conv_b, rg_wa, rg_ba, rg_wx, rg_bx, rg_lambda, rg_w_out, cv_w_pw1, cv_b_pw1, cv_dw_w, cv_dw_b, cv_ln_g, cv_ln_b, cv_w_pw2, cv_b_pw2, at_w_qkv, at_q_norm, at_k_norm, at_lam_q1, at_lam_k1, at_lam_q2, at_lam_k2, at_subln, at_w_o):
    bp, tp, d = x_prompt.shape
    bs, ts, _ = x_sample.shape
    nh = d // HEAD_WIDTH
    past = page_table.shape[1] * cache_k.shape[1]
    bf = lambda w: w.astype(BF16)
    row = lambda v: v.reshape(1, -1).astype(F32)
    rows = lambda v: v.reshape(v.shape[0], 1, -1).astype(F32)

    seg_id = jnp.arange(HEAD_WIDTH) // HEAD_DIM
    p = dict(
        ffn1_norm=rows(ffn1_norm), ffn1_w_gate=bf(ffn1_w_gate), ffn1_w_up=bf(ffn1_w_up), ffn1_w_down=bf(ffn1_w_down),
        mix_norm=rows(mix_norm),
        ffn2_norm=rows(ffn2_norm), ffn2_w_gate=bf(ffn2_w_gate), ffn2_w_up=bf(ffn2_w_up), ffn2_w_down=bf(ffn2_w_down),
        pool_w=bf(pool_w), pool_scale=row(pool_scale),
        rg_w_gate=bf(rg_w_gate), rg_w_in=bf(rg_w_in), rg_conv_w=rg_conv_w, rg_conv_b=row(rg_conv_b),
        rg_wa=bf(rg_wa), rg_ba=row(rg_ba), rg_wx=bf(rg_wx), rg_bx=row(rg_bx), rg_lambda=row(rg_lambda),
        rg_w_out=bf(rg_w_out),
        cv_w_pw1=bf(cv_w_pw1), cv_b_pw1=row(cv_b_pw1), cv_dw_w=cv_dw_w, cv_dw_b=row(cv_dw_b), cv_ln_g=row(cv_ln_g),
        cv_ln_b=row(cv_ln_b), cv_w_pw2=bf(cv_w_pw2), cv_b_pw2=row(cv_b_pw2),
        at_w_qkv=bf(at_w_qkv), at_q_norm=row(jnp.tile(at_q_norm, 2)), at_k_norm=row(jnp.tile(at_k_norm, 2)),
        at_w_o=bf(at_w_o),
        seg=((seg_id[:, None] == seg_id[None, :]).astype(F32) / HEAD_DIM).astype(BF16),
    )
    subln = row(at_subln)
    lam = pl.pallas_call(
        _lam_kernel,
        in_specs=[_const_spec((1, HEAD_DIM))] * 4,
        out_specs=_const_spec((1, 1)),
        out_shape=jax.ShapeDtypeStruct((1, 1), F32),
        name="diff_lambda",
    )(row(at_lam_q1), row(at_lam_k1), row(at_lam_q2), row(at_lam_k2))

    zeros = lambda n: jnp.zeros((bp, n, d), F32)
    prompt = dict(x=x_prompt, stride=1, pos0=0,
                  state=dict(pool=zeros(POOL_BUF), rgc=zeros(RG_CONV - 1), rgh=zeros(1), cv=zeros(CV_WIDTH - 1)))

    def to_tm(a):
        return jnp.swapaxes(a, 0, 1).reshape(1, a.shape[1] * bs, d)

    def from_tm(a):
        return jnp.swapaxes(a.reshape(-1, bs, d), 0, 1)

    def attend(qkvs):
        (qp, _, _, kbp, vtp), (qs, ks, vs) = qkvs
        q5 = from_tm(qs).reshape(bs, ts, nh, 2, HEAD_DIM)
        eye = jnp.eye(2, dtype=qs.dtype)
        qm = jnp.einsum('bqhcd,ce->bhcqed', q5, eye).reshape(bs, nh * 2 * ts, HEAD_WIDTH)
        k_new = from_tm(ks).reshape(bs, ts, nh, HEAD_WIDTH)
        v_new = from_tm(vs).reshape(bs, ts, nh, HEAD_WIDTH)
        shape = (bp, tp, d)
        heads_p, heads_s = _attention(page_table, lam, subln, qp.reshape(shape), kbp.reshape(shape), vtp, qm, k_new,
                                      v_new, cache_k, cache_v)
        return [heads_p.reshape(-1, d), jnp.transpose(heads_s, (2, 0, 1, 3)).reshape(ts * bs, d)]

    sample = dict(x=to_tm(x_sample), stride=bs, pos0=past,
                  state=dict(pool=to_tm(state_pool), rgc=to_tm(state_rglru_conv),
                             rgh=state_rglru_h.reshape(1, bs, d), cv=to_tm(state_conv)))
    (y_p, y_s), (o_p, o_s) = _trunk([prompt, sample], attend, p)

    kv4 = lambda a, b, t: a.reshape(b, t, nh, HEAD_WIDTH)
    return (y_p, from_tm(y_s),
            o_p['pool'], from_tm(o_s['pool']),
            o_p['rgc'], from_tm(o_s['rgc']),
            o_p['rgh'].reshape(bp, d), o_s['rgh'].reshape(bs, d),
            o_p['cv'], from_tm(o_s['cv']),
            kv4(o_p['k'], bp, tp), kv4(o_p['v'], bp, tp),
            kv4(from_tm(o_s['k']), bs, ts), kv4(from_tm(o_s['v']), bs, ts))
```
